```python
import math
import jax, jax.numpy as jnp
from jax import lax
import numpy as np

D_MODEL = 2048
BATCH = 16
SEQ = 2048
DEPTH = 1
DEC_BATCH = 32
DEC_SEQ = 64
PAST_LEN = 4096

CHUNK = 64
SUB = 16
N_SUB = CHUNK // SUB
D_FF = 5632
PLE_DIM = 256
RG_WIDTH = D_MODEL
RG_HEADS = 16
RG_HEAD_DIM = RG_WIDTH // RG_HEADS
CONV_W = 4
RG_C = 8.0
HG_HEADS = 16
HG_DK = 128
HG_DV = D_MODEL // HG_HEADS
HG_WIDTH = HG_HEADS * HG_DK
EPS = 1e-6
SPLIT_POINTS = [RG_WIDTH, 2 * RG_WIDTH, 2 * RG_WIDTH + HG_WIDTH, 2 * RG_WIDTH + 2 * HG_WIDTH,
                2 * RG_WIDTH + 2 * HG_WIDTH + D_MODEL, 2 * RG_WIDTH + 2 * HG_WIDTH + 2 * D_MODEL,
                2 * RG_WIDTH + 2 * HG_WIDTH + 3 * D_MODEL]
W_IN_COLS = 2 * RG_WIDTH + 2 * HG_WIDTH + 4 * D_MODEL

kernel_name = 'hawk_hgrn2_gated_macaron_stream_step'


def _rms_norm(x, g):
    xf = x.astype(jnp.float32)
    y = xf * lax.rsqrt(jnp.mean(xf * xf, axis=-1, keepdims=True) + EPS)
    return (y * g.astype(jnp.float32)).astype(x.dtype)


def _swiglu(u, w_in, w_out):
    gate, up = jnp.split(u @ w_in, 2, axis=-1)
    return (jax.nn.silu(gate) * up) @ w_out


def _causal_conv(x, buf, w, b):
    t = x.shape[1]
    xp = jnp.concatenate([buf.astype(x.dtype), x], axis=1)
    out = b + w[0] * xp[:, CONV_W - 1:CONV_W - 1 + t]
    for j in range(1, CONV_W):
        out = out + w[j] * xp[:, CONV_W - 1 - j:CONV_W - 1 - j + t]
    return out, xp[:, xp.shape[1] - (CONV_W - 1):]


def _block_diag(x, w, b):
    bsz, t, _ = x.shape
    xh = x.reshape(bsz, t, RG_HEADS, RG_HEAD_DIM)
    return (jnp.einsum('bthi,hij->bthj', xh, w) + b).reshape(bsz, t, RG_WIDTH)


def _lin_combine(left, right):
    a1, b1 = left
    a2, b2 = right
    return a1 * a2, a2 * b1 + b2


def _rg_lru(x, h0, w_a, b_a, w_x, b_x, a_param, pos0):
    t = x.shape[1]
    f32 = jnp.float32
    xf = x.astype(f32)
    r = jax.nn.sigmoid(_block_diag(xf, w_a.astype(f32), b_a.astype(f32)))
    i = jax.nn.sigmoid(_block_diag(xf, w_x.astype(f32), b_x.astype(f32)))
    log_a = -RG_C * r * jax.nn.softplus(a_param.astype(f32))
    reset = ((pos0 + jnp.arange(t)) == 0)[None, :, None]
    a = jnp.where(reset, 0.0, jnp.exp(log_a))
    mult = jnp.where(reset, 1.0, jnp.sqrt(-jnp.expm1(2.0 * log_a)))
    u = mult * i * xf
    u = u.at[:, 0].add(a[:, 0] * h0.astype(f32))
    _, h = lax.associative_scan(_lin_combine, (a, u), axis=1)
    return h.astype(x.dtype), h[:, -1].astype(x.dtype)


def _hgrn2_chunk(s, inp):
    q, lf, k, v = inp
    bsz = q.shape[0]
    b = jnp.cumsum(lf, axis=2)
    o = jnp.einsum('bhtk,bhkv->bhtv', q * jnp.exp(b), s)
    qs = q.reshape(bsz, HG_HEADS, N_SUB, SUB, HG_DK)
    ks = k.reshape(bsz, HG_HEADS, N_SUB, SUB, HG_DK)
    bs = b.reshape(bsz, HG_HEADS, N_SUB, SUB, HG_DK)
    tri = jnp.tril(jnp.ones((SUB, SUB), bool))[:, :, None]
    diff = bs[:, :, :, :, None, :] - bs[:, :, :, None, :, :]
    dec = jnp.exp(jnp.where(tri, diff, -jnp.inf))
    a_diag = jnp.einsum('bhntk,bhnsk,bhntsk->bhnts', qs, ks, dec)
    ref = bs[:, :, :, -1, :]
    later = jnp.arange(N_SUB)[:, None] > jnp.arange(N_SUB)[None, :]
    e_q = bs[:, :, :, :, None, :] - ref[:, :, None, None, :, :]
    q_off = qs[:, :, :, :, None, :] * jnp.exp(jnp.where(later[:, None, :, None], e_q, -jnp.inf))
    k_off = ks * jnp.exp(ref[:, :, :, None, :] - bs)
    a_off = jnp.einsum('bhitjk,bhjsk->bhitjs', q_off, k_off)
    a_full = a_off + jnp.einsum('bhnts,nm->bhntms', a_diag, jnp.eye(N_SUB, dtype=a_diag.dtype))
    a_full = a_full.reshape(bsz, HG_HEADS, CHUNK, CHUNK)
    o = o + jnp.einsum('bhts,bhsv->bhtv', a_full, v)
    b_last = b[:, :, -1:, :]
    s_new = jnp.exp(b_last[:, :, 0, :])[..., None] * s + jnp.einsum(
        'bhsk,bhsv->bhkv', k * jnp.exp(b_last - b), v)
    return s_new, o


def _hgrn2(q, lf, k, v, s0):
    bsz, t, _ = q.shape
    n_chunks = -(-t // CHUNK)
    pad = n_chunks * CHUNK - t

    def prep(z, d):
        z = jnp.pad(z, ((0, 0), (0, pad), (0, 0)))
        return z.reshape(bsz, n_chunks, CHUNK, HG_HEADS, d).transpose(1, 0, 3, 2, 4)

    s, o = lax.scan(_hgrn2_chunk, s0, (prep(q, HG_DK), prep(lf, HG_DK), prep(k, HG_DK), prep(v, HG_DV)))
    o = o.transpose(1, 0, 3, 2, 4).reshape(bsz, n_chunks * CHUNK, HG_HEADS, HG_DV)[:, :t]
    return o, s


def _layer(x, p, conv_buf, h0, s0, pos0, lb, lw):
    (g_ffn1_pre, g_ffn1_post, w_ffn1_in, w_ffn1_out, g_mix_pre, g_mix_post, w_in,
     rg_conv_w, rg_conv_b, rg_w_a, rg_b_a, rg_w_x, rg_b_x, rg_a_param, hg_norm_g,
     w_branch_a, w_branch_b, w_out, g_ffn2_pre, g_ffn2_post, w_ffn2_in, w_ffn2_out,
     g_ple_pre, g_ple_post, w_ple_gate, w_ple_proj) = lw
    f32 = jnp.float32
    bsz, t, _ = x.shape
    h = x
    h = h + 0.5 * _rms_norm(_swiglu(_rms_norm(h, g_ffn1_pre), w_ffn1_in, w_ffn1_out), g_ffn1_post)
    u = _rms_norm(h, g_mix_pre)
    rg_x, rg_y, hg_q, hg_f, hg_i, hg_g, gate_a, gate_b = jnp.split(u @ w_in, SPLIT_POINTS, axis=-1)
    xc, conv_new = _causal_conv(rg_x, conv_buf, rg_conv_w, rg_conv_b)
    hr, h_last = _rg_lru(xc, h0, rg_w_a, rg_b_a, rg_w_x, rg_b_x, rg_a_param, pos0)
    br_a = (jax.nn.gelu(rg_y) * hr) @ w_branch_a
    f = lb + (1.0 - lb) * jax.nn.sigmoid(hg_f.astype(f32))
    q = jax.nn.silu(hg_q.astype(f32))
    o, s_new = _hgrn2(q, jnp.log(f), 1.0 - f, hg_i.astype(f32), s0.astype(f32))
    o = _rms_norm(o, hg_norm_g.reshape(HG_HEADS, HG_DV)).reshape(bsz, t, D_MODEL)
    br_b = (o * jax.nn.silu(hg_g.astype(f32))).astype(x.dtype) @ w_branch_b
    m = jax.nn.sigmoid(gate_a) * br_a + jax.nn.sigmoid(gate_b) * br_b
    h = h + _rms_norm(m @ w_out, g_mix_post)
    h = h + 0.5 * _rms_norm(_swiglu(_rms_norm(h, g_ffn2_pre), w_ffn2_in, w_ffn2_out), g_ffn2_post)
    e = (p.astype(x.dtype) @ w_ple_proj) * jax.nn.sigmoid(_rms_norm(h, g_ple_pre) @ w_ple_gate)
    h = h + _rms_norm(e, g_ple_post)
    return h, conv_new, h_last, s_new.astype(x.dtype)


def setup_inputs(seed: int = 0) -> dict:
    key = jax.random.key(seed)
    ks = jax.random.split(key, 34)
    f32 = jnp.float32

    def nrm(k, shape, scale):
        return jax.random.normal(k, shape, f32) * scale

    def gain(k):
        return 1.0 + 0.05 * jax.random.normal(k, (DEPTH, D_MODEL), f32)

    u = jax.random.uniform(ks[20], (DEPTH, RG_WIDTH), f32, minval=0.9, maxval=0.999)
    a_param = jnp.log(jnp.expm1(-jnp.log(u) / RG_C))
    return {
        'x_prompt': nrm(ks[0], (BATCH, SEQ, D_MODEL), 1.0),
        'x_sample': nrm(ks[1], (DEC_BATCH, DEC_SEQ, D_MODEL), 1.0),
        'p_prompt': nrm(ks[2], (DEPTH, BATCH, SEQ, PLE_DIM), 1.0),
        'p_sample': nrm(ks[3], (DEPTH, DEC_BATCH, DEC_SEQ, PLE_DIM), 1.0),
        'state_rglru_conv': nrm(ks[4], (DEPTH, DEC_BATCH, CONV_W - 1, RG_WIDTH), 1.0),
        'state_rglru_h': nrm(ks[5], (DEPTH, DEC_BATCH, RG_WIDTH), 0.5),
        'state_hgrn2': nrm(ks[6], (DEPTH, DEC_BATCH, HG_HEADS, HG_DK, HG_DV), 0.5),
        'g_ffn1_pre': gain(ks[7]),
        'g_ffn1_post': gain(ks[8]),
        'w_ffn1_in': nrm(ks[9], (DEPTH, D_MODEL, 2 * D_FF), D_MODEL ** -0.5),
        'w_ffn1_out': nrm(ks[10], (DEPTH, D_FF, D_MODEL), D_FF ** -0.5),
        'g_mix_pre': gain(ks[11]),
        'g_mix_post': gain(ks[12]),
        'w_in': nrm(ks[13], (DEPTH, D_MODEL, W_IN_COLS), D_MODEL ** -0.5),
        'rg_conv_w': nrm(ks[14], (DEPTH, CONV_W, RG_WIDTH), CONV_W ** -0.5),
        'rg_conv_b': nrm(ks[15], (DEPTH, RG_WIDTH), 0.01),
        'rg_w_a': nrm(ks[16], (DEPTH, RG_HEADS, RG_HEAD_DIM, RG_HEAD_DIM), RG_HEAD_DIM ** -0.5),
        'rg_b_a': nrm(ks[17], (DEPTH, RG_HEADS, RG_HEAD_DIM), 0.01),
        'rg_w_x': nrm(ks[18], (DEPTH, RG_HEADS, RG_HEAD_DIM, RG_HEAD_DIM), RG_HEAD_DIM ** -0.5),
        'rg_b_x': nrm(ks[19], (DEPTH, RG_HEADS, RG_HEAD_DIM), 0.01),
        'rg_a_param': a_param,
        'hg_lower_bound': nrm(ks[21], (DEPTH + 1, HG_WIDTH), 0.1),
        'hg_norm_g': gain(ks[22]),
        'w_branch_a': nrm(ks[23], (DEPTH, RG_WIDTH, D_MODEL), RG_WIDTH ** -0.5),
        'w_branch_b': nrm(ks[24], (DEPTH, D_MODEL, D_MODEL), D_MODEL ** -0.5),
        'w_out': nrm(ks[25], (DEPTH, D_MODEL, D_MODEL), D_MODEL ** -0.5),
        'g_ffn2_pre': gain(ks[26]),
        'g_ffn2_post': gain(ks[27]),
        'w_ffn2_in': nrm(ks[28], (DEPTH, D_MODEL, 2 * D_FF), D_MODEL ** -0.5),
        'w_ffn2_out': nrm(ks[29], (DEPTH, D_FF, D_MODEL), D_FF ** -0.5),
        'g_ple_pre': gain(ks[30]),
        'g_ple_post': gain(ks[31]),
        'w_ple_gate': nrm(ks[32], (DEPTH, D_MODEL, D_MODEL), D_MODEL ** -0.5),
        'w_ple_proj': nrm(ks[33], (DEPTH, PLE_DIM, D_MODEL), PLE_DIM ** -0.5),
    }


def reference(x_prompt, x_sample, p_prompt, p_sample, state_rglru_conv, state_rglru_h, state_hgrn2,
              g_ffn1_pre, g_ffn1_post, w_ffn1_in, w_ffn1_out, g_mix_pre, g_mix_post, w_in,
              rg_conv_w, rg_conv_b, rg_w_a, rg_b_a, rg_w_x, rg_b_x, rg_a_param, hg_lower_bound,
              hg_norm_g, w_branch_a, w_branch_b, w_out, g_ffn2_pre, g_ffn2_post, w_ffn2_in,
              w_ffn2_out, g_ple_pre, g_ple_post, w_ple_gate, w_ple_proj):
    lb_all = jnp.cumsum(jax.nn.softmax(hg_lower_bound.astype(jnp.float32), axis=0), axis=0)
    yp, ys = x_prompt, x_sample
    bp = x_prompt.shape[0]
    conv_p, h_p, s_p, conv_s, h_s, s_s = [], [], [], [], [], []
    for l in range(DEPTH):
        lw = (g_ffn1_pre[l], g_ffn1_post[l], w_ffn1_in[l], w_ffn1_out[l], g_mix_pre[l], g_mix_post[l],
              w_in[l], rg_conv_w[l], rg_conv_b[l], rg_w_a[l], rg_b_a[l], rg_w_x[l], rg_b_x[l],
              rg_a_param[l], hg_norm_g[l], w_branch_a[l], w_branch_b[l], w_out[l], g_ffn2_pre[l],
              g_ffn2_post[l], w_ffn2_in[l], w_ffn2_out[l], g_ple_pre[l], g_ple_post[l],
              w_ple_gate[l], w_ple_proj[l])
        yp, c, hh, s = _layer(yp, p_prompt[l],
                              jnp.zeros((bp, CONV_W - 1, RG_WIDTH), yp.dtype),
                              jnp.zeros((bp, RG_WIDTH), yp.dtype),
                              jnp.zeros((bp, HG_HEADS, HG_DK, HG_DV), jnp.float32),
                              0, lb_all[l], lw)
        conv_p.append(c)
        h_p.append(hh)
        s_p.append(s)
        ys, c, hh, s = _layer(ys, p_sample[l], state_rglru_conv[l], state_rglru_h[l], state_hgrn2[l],
                              PAST_LEN, lb_all[l], lw)
        conv_s.append(c)
        h_s.append(hh)
        s_s.append(s)
    return (yp, ys, jnp.stack(conv_p), jnp.stack(h_p), jnp.stack(s_p),
            jnp.stack(conv_s), jnp.stack(h_s), jnp.stack(s_s))
```

```python
import functools

import jax
import jax.numpy as jnp
from jax import lax
from jax.experimental import pallas as pl
from jax.experimental.pallas import tpu as pltpu

F32 = jnp.float32
BF16 = jnp.bfloat16

EPS = 1e-6
RG_C = 8.0
CONV_W = 4
HEAD = 128
CHUNK = 64
SUB = 16
N_SUB = CHUNK // SUB
SUBLANES = 8
VMEM_LIMIT = 56 * 1024 * 1024


def _cparams(sem):
    return pltpu.CompilerParams(dimension_semantics=sem, vmem_limit_bytes=VMEM_LIMIT)


def _rms(x, g):
    return x * lax.rsqrt(jnp.mean(x * x, axis=-1, keepdims=True) + EPS) * g


def _sigmoid(x):
    return 1.0 / (1.0 + jnp.exp(-x))


def _silu(x):
    return x * _sigmoid(x)


def _gelu_tanh(x):
    return 0.5 * x * (1.0 + jnp.tanh(0.7978845608028654 * (x + 0.044715 * (x * x * x))))


def _log1p(z):
    w = 1.0 + z
    return jnp.where(w == 1.0, z, jnp.log(w) * (z / jnp.where(w == 1.0, 1.0, w - 1.0)))


def _pick(n, prefs):
    for p in prefs:
        if n % p == 0:
            return p
    raise ValueError(f"no tile for {n} in {prefs}")


def _ffn_kernel(x_ref, gpre_ref, gpost_ref, wg_ref, wu_ref, wo_ref, o_ref, xn_ref, acc_ref):
    j = pl.program_id(1)

    @pl.when(j == 0)
    def _():
        xn_ref[...] = _rms(x_ref[...], gpre_ref[...]).astype(BF16)
        acc_ref[...] = jnp.zeros_like(acc_ref)

    xn = xn_ref[...]
    gate = jnp.dot(xn, wg_ref[...], preferred_element_type=F32)
    up = jnp.dot(xn, wu_ref[...], preferred_element_type=F32)
    act = (_silu(gate) * up).astype(BF16)
    acc_ref[...] += jnp.dot(act, wo_ref[...], preferred_element_type=F32)

    @pl.when(j == pl.num_programs(1) - 1)
    def _():
        o_ref[...] = x_ref[...] + 0.5 * _rms(acc_ref[...], gpost_ref[...])


def _ffn(x, g_pre, g_post, w_in, w_out):
    m, d = x.shape
    f = w_out.shape[0]
    tm = _pick(m, (512, 256, 128))
    tf = _pick(f, (512, 256, 128))
    nf = f // tf
    return pl.pallas_call(
        _ffn_kernel,
        grid=(m // tm, nf),
        in_specs=[
            pl.BlockSpec((tm, d), lambda i, j: (i, 0)),
            pl.BlockSpec((1, d), lambda i, j: (0, 0)),
            pl.BlockSpec((1, d), lambda i, j: (0, 0)),
            pl.BlockSpec((d, tf), lambda i, j: (0, j)),
            pl.BlockSpec((d, tf), lambda i, j: (0, j + nf)),
            pl.BlockSpec((tf, d), lambda i, j: (j, 0)),
        ],
        out_specs=pl.BlockSpec((tm, d), lambda i, j: (i, 0)),
        out_shape=jax.ShapeDtypeStruct((m, d), F32),
        scratch_shapes=[pltpu.VMEM((tm, d), BF16), pltpu.VMEM((tm, d), F32)],
        compiler_params=_cparams(("parallel", "arbitrary")),
        name="ffn",
    )(x, g_pre, g_post, w_in, w_in, w_out)


def _proj_kernel(x_ref, g_ref, w_ref, o_ref, xn_ref):
    @pl.when(pl.program_id(1) == 0)
    def _():
        xn_ref[...] = _rms(x_ref[...], g_ref[...]).astype(BF16)

    o_ref[...] = jnp.dot(xn_ref[...], w_ref[...], preferred_element_type=F32).astype(o_ref.dtype)


def _proj(x, g, w):
    m, d = x.shape
    n = w.shape[1]
    tm = _pick(m, (512, 256, 128))
    tn = _pick(n, (1024, 512, 256, 128))
    return pl.pallas_call(
        _proj_kernel,
        grid=(m // tm, n // tn),
        in_specs=[
            pl.BlockSpec((tm, d), lambda i, j: (i, 0)),
            pl.BlockSpec((1, d), lambda i, j: (0, 0)),
            pl.BlockSpec((d, tn), lambda i, j: (0, j)),
        ],
        out_specs=pl.BlockSpec((tm, tn), lambda i, j: (i, j)),
        out_shape=jax.ShapeDtypeStruct((m, n), BF16),
        scratch_shapes=[pltpu.VMEM((tm, d), BF16)],
        compiler_params=_cparams(("parallel", "arbitrary")),
        name="proj",
    )(x, g, w)


def _rglru_kernel(x_ref, y_ref, cbuf_ref, h0_ref, cw_ref, cb_ref, wax_ref, bax_ref, ap_ref,
                  o_ref, hlast_ref, xp_ref, a_ref, u_ref, carry_ref, *, tb, cols, reset_first):
    t = pl.program_id(2)
    pad = SUBLANES

    @pl.when(t == 0)
    def _():
        xp_ref[0:pad, :] = cbuf_ref[0]
        carry_ref[...] = h0_ref[0]

    xp_ref[pad:pad + tb, :] = x_ref[...].astype(F32)
    cw = cw_ref[...]
    xc = cb_ref[...] + cw[0:1, :] * xp_ref[pad:pad + tb, :]
    for j in range(1, CONV_W):
        xc = xc + cw[j:j + 1, :] * xp_ref[pad - j:pad - j + tb, :]
    xp_ref[0:pad, :] = xp_ref[tb:tb + pad, :]

    softplus_a = jnp.maximum(ap_ref[...], 0.0) + _log1p(jnp.exp(-jnp.abs(ap_ref[...])))
    if reset_first:
        row = lax.broadcasted_iota(jnp.int32, (tb, HEAD), 0)
        is_reset = jnp.logical_and(row == 0, t == 0)
    for h in range(cols // HEAD):
        sl = slice(h * HEAD, (h + 1) * HEAD)
        xh = xc[:, sl]
        gates = jnp.dot(xh.astype(BF16), wax_ref[h], preferred_element_type=F32) + bax_ref[h]
        r = _sigmoid(gates[:, :HEAD])
        i = _sigmoid(gates[:, HEAD:])
        log_a = (-RG_C) * r * softplus_a[:, sl]
        a = jnp.exp(log_a)
        mult = jnp.sqrt(1.0 - a * a)
        if reset_first:
            a = jnp.where(is_reset, 0.0, a)
            mult = jnp.where(is_reset, 1.0, mult)
        a_ref[:, sl] = a
        u_ref[:, sl] = mult * i * xh

    srow = lax.broadcasted_iota(jnp.int32, (SUBLANES, cols), 0)

    def body(g, carry):
        r0 = pl.multiple_of(g * SUBLANES, SUBLANES)
        a = a_ref[pl.ds(r0, SUBLANES), :]
        u = u_ref[pl.ds(r0, SUBLANES), :]
        for dd in (1, 2, 4):
            keep = srow >= dd
            u = jnp.where(keep, a * pltpu.roll(u, dd, 0) + u, u)
            a = jnp.where(keep, a * pltpu.roll(a, dd, 0), a)
        hcur = a * carry + u
        u_ref[pl.ds(r0, SUBLANES), :] = hcur
        return jnp.broadcast_to(hcur[SUBLANES - 1:SUBLANES, :], (SUBLANES, cols))

    carry = lax.fori_loop(0, tb // SUBLANES, body,
                          jnp.broadcast_to(carry_ref[...], (SUBLANES, cols)))
    carry_ref[...] = carry[0:1, :]
    hlast_ref[0] = carry[0:1, :]
    o_ref[...] = (_gelu_tanh(y_ref[...].astype(F32)) * u_ref[...]).astype(o_ref.dtype)


def _rglru(proj, nb, t, conv_buf, h0, cw, cb, wax, bax, a_param, reset_first):
    d = cw.shape[1]
    cols = 512
    hpc = cols // HEAD
    tb = _pick(t, (512, 256, 128, 64))
    nt = t // tb
    ncol = d // cols
    kern = functools.partial(_rglru_kernel, tb=tb, cols=cols, reset_first=reset_first)
    return pl.pallas_call(
        kern,
        grid=(nb, ncol, nt),
        in_specs=[
            pl.BlockSpec((tb, cols), lambda b, c, s: (b * nt + s, c)),
            pl.BlockSpec((tb, cols), lambda b, c, s: (b * nt + s, ncol + c)),
            pl.BlockSpec((1, SUBLANES, cols), lambda b, c, s: (b, 0, c)),
            pl.BlockSpec((1, 1, cols), lambda b, c, s: (b, 0, c)),
            pl.BlockSpec((CONV_W, cols), lambda b, c, s: (0, c)),
            pl.BlockSpec((1, cols), lambda b, c, s: (0, c)),
            pl.BlockSpec((hpc, HEAD, 2 * HEAD), lambda b, c, s: (c, 0, 0)),
            pl.BlockSpec((hpc, 1, 2 * HEAD), lambda b, c, s: (c, 0, 0)),
            pl.BlockSpec((1, cols), lambda b, c, s: (0, c)),
        ],
        out_specs=[
            pl.BlockSpec((tb, cols), lambda b, c, s: (b * nt + s, c)),
            pl.BlockSpec((1, 1, cols), lambda b, c, s: (b, 0, c)),
        ],
        out_shape=[
            jax.ShapeDtypeStruct((nb * t, d), BF16),
            jax.ShapeDtypeStruct((nb, 1, d), F32),
        ],
        scratch_shapes=[
            pltpu.VMEM((tb + SUBLANES, cols), F32),
            pltpu.VMEM((tb, cols), F32),
            pltpu.VMEM((tb, cols), F32),
            pltpu.VMEM((1, cols), F32),
        ],
        compiler_params=_cparams(("parallel", "parallel", "arbitrary")),
        name="rglru",
    )(proj, proj, conv_buf, h0, cw, cb, wax, bax, a_param)


def _cumsum_rows(x, row):
    n = x.shape[0]
    d = 1
    while d < n:
        x = x + jnp.where(row >= d, pltpu.roll(x, d, 0), 0.0)
        d *= 2
    return x


def _hgrn2_kernel(q_ref, f_ref, v_ref, g_ref, s0_ref, lbp_ref, gn_ref, o_ref, snew_ref,
                  st_ref, kpad_ref, bpad_ref, w_ref, *, tb):
    t = pl.program_id(2)

    @pl.when(t == 0)
    def _():
        st_ref[...] = s0_ref[0, 0].T
        kpad_ref[0:SUB, :] = jnp.zeros((SUB, HEAD), F32)
        bpad_ref[0:SUB, :] = jnp.zeros((SUB, HEAD), F32)

    lbp = lbp_ref[...]
    lmax = jnp.max(lbp, axis=0, keepdims=True)
    le = jnp.exp(lbp - lmax)
    lb = le[0:1, :] / jnp.sum(le, axis=0, keepdims=True)
    gn = gn_ref[...]
    row = lax.broadcasted_iota(jnp.int32, (CHUNK, HEAD), 0)
    col = lax.broadcasted_iota(jnp.int32, (CHUNK, HEAD), 1)
    sub = jnp.bitwise_and(row, SUB - 1)
    ones = jnp.ones((HEAD, HEAD), BF16)
    nt_dims = (((1,), (1,)), ((), ()))

    def chunk(c, carry):
        r0 = pl.multiple_of(c * CHUNK, CHUNK)
        qp = q_ref[pl.ds(r0, CHUNK), :].astype(F32)
        fp = f_ref[pl.ds(r0, CHUNK), :].astype(F32)
        v = v_ref[pl.ds(r0, CHUNK), :]
        gp = g_ref[pl.ds(r0, CHUNK), :].astype(F32)
        f = lb + (1.0 - lb) * _sigmoid(fp)
        k = 1.0 - f
        q = _silu(qp)
        b = _cumsum_rows(jnp.log(f), row)
        st = st_ref[...]
        o = lax.dot_general((q * jnp.exp(b)).astype(BF16), st.astype(BF16), nt_dims,
                            preferred_element_type=F32)
        ends = [b[SUB * j + SUB - 1:SUB * j + SUB, :] for j in range(N_SUB)]
        bend = jnp.concatenate([jnp.broadcast_to(e, (SUB, HEAD)) for e in ends], axis=0)
        koff = k * jnp.exp(bend - b)
        qs, ks = [], []
        for j in range(N_SUB - 1):
            later = row >= SUB * (j + 1)
            qs.append(jnp.where(later, q * jnp.exp(jnp.where(later, b - ends[j], 0.0)), 0.0))
            ks.append(jnp.where(jnp.logical_and(row >= SUB * j, row < SUB * (j + 1)), koff, 0.0))
        a_off = lax.dot_general(jnp.concatenate(qs, axis=1).astype(BF16),
                                jnp.concatenate(ks, axis=1).astype(BF16), nt_dims,
                                preferred_element_type=F32)
        kpad_ref[SUB:SUB + CHUNK, :] = k
        bpad_ref[SUB:SUB + CHUNK, :] = b
        for lag in range(SUB):
            kk = kpad_ref[SUB - lag:SUB - lag + CHUNK, :]
            bb = bpad_ref[SUB - lag:SUB - lag + CHUNK, :]
            ok = sub >= lag
            w = jnp.where(ok, q * kk * jnp.exp(jnp.where(ok, b - bb, 0.0)), 0.0)
            w_ref[lag * CHUNK:(lag + 1) * CHUNK, :] = w.astype(BF16)
        rs = jnp.dot(w_ref[...], ones, preferred_element_type=F32)
        a_diag = jnp.zeros((CHUNK, HEAD), F32)
        for lag in range(SUB):
            a_diag = jnp.where(col == row - lag, rs[lag * CHUNK:(lag + 1) * CHUNK, :], a_diag)
        a_full = a_off + a_diag[:, :CHUNK]
        o = o + jnp.dot(a_full.astype(BF16), v, preferred_element_type=F32)
        blast = b[CHUNK - 1:CHUNK, :]
        kd = (k * jnp.exp(blast - b)).astype(BF16)
        st_ref[...] = st * jnp.exp(blast) + jnp.dot(v.astype(F32).T.astype(BF16), kd,
                                                      preferred_element_type=F32)
        o_ref[pl.ds(r0, CHUNK), :] = (_rms(o, gn) * _silu(gp)).astype(o_ref.dtype)
        return carry

    lax.fori_loop(0, tb // CHUNK, chunk, 0)
    snew_ref[0, 0] = st_ref[...].T


def _hgrn2(proj, nb, t, s0, lb_param, gn):
    nh = s0.shape[1]
    d = nh * HEAD
    tb = _pick(t, (1024, 512, 256, 128, 64))
    nt = t // tb
    kern = functools.partial(_hgrn2_kernel, tb=tb)

    def col_spec(group):
        return pl.BlockSpec((tb, HEAD), lambda b, h, s: (b * nt + s, group * nh + h))

    return pl.pallas_call(
        kern,
        grid=(nb, nh, nt),
        in_specs=[
            col_spec(2), col_spec(3), col_spec(4), col_spec(5),
            pl.BlockSpec((1, 1, HEAD, HEAD), lambda b, h, s: (b, h, 0, 0)),
            pl.BlockSpec((lb_param.shape[0], HEAD), lambda b, h, s: (0, h)),
            pl.BlockSpec((1, HEAD), lambda b, h, s: (0, h)),
        ],
        out_specs=[
            pl.BlockSpec((tb, HEAD), lambda b, h, s: (b * nt + s, h)),
            pl.BlockSpec((1, 1, HEAD, HEAD), lambda b, h, s: (b, h, 0, 0)),
        ],
        out_shape=[
            jax.ShapeDtypeStruct((nb * t, d), BF16),
            jax.ShapeDtypeStruct(s0.shape, F32),
        ],
        scratch_shapes=[
            pltpu.VMEM((HEAD, HEAD), F32),
            pltpu.VMEM((SUB + CHUNK, HEAD), F32),
            pltpu.VMEM((SUB + CHUNK, HEAD), F32),
            pltpu.VMEM((SUB * CHUNK, HEAD), BF16),
        ],
        compiler_params=_cparams(("parallel", "parallel", "arbitrary")),
        name="hgrn2",
    )(proj, proj, proj, proj, s0, lb_param, gn)


def _merge_kernel(a_ref, b_ref, ga_ref, gb_ref, h_ref, g_ref, wa_ref, wb_ref, wo_ref, o_ref, acc_ref):
    j = pl.program_id(1)

    @pl.when(j == 0)
    def _():
        acc_ref[...] = jnp.zeros_like(acc_ref)

    br_a = jnp.dot(a_ref[...], wa_ref[...], preferred_element_type=F32)
    br_b = jnp.dot(b_ref[...], wb_ref[...], preferred_element_type=F32)
    m = _sigmoid(ga_ref[...].astype(F32)) * br_a + _sigmoid(gb_ref[...].astype(F32)) * br_b
    acc_ref[...] += jnp.dot(m.astype(BF16), wo_ref[...], preferred_element_type=F32)

    @pl.when(j == pl.num_programs(1) - 1)
    def _():
        o_ref[...] = h_ref[...] + _rms(acc_ref[...], g_ref[...])


def _merge(a_in, b_in, proj, h, g_post, w_a, w_b, w_o):
    m, d = h.shape
    tm = _pick(m, (512, 256, 128))
    tn = 512
    nn = d // tn
    return pl.pallas_call(
        _merge_kernel,
        grid=(m // tm, nn),
        in_specs=[
            pl.BlockSpec((tm, d), lambda i, j: (i, 0)),
            pl.BlockSpec((tm, d), lambda i, j: (i, 0)),
            pl.BlockSpec((tm, tn), lambda i, j: (i, 6 * nn + j)),
            pl.BlockSpec((tm, tn), lambda i, j: (i, 7 * nn + j)),
            pl.BlockSpec((tm, d), lambda i, j: (i, 0)),
            pl.BlockSpec((1, d), lambda i, j: (0, 0)),
            pl.BlockSpec((d, tn), lambda i, j: (0, j)),
            pl.BlockSpec((d, tn), lambda i, j: (0, j)),
            pl.BlockSpec((tn, d), lambda i, j: (j, 0)),
        ],
        out_specs=pl.BlockSpec((tm, d), lambda i, j: (i, 0)),
        out_shape=jax.ShapeDtypeStruct((m, d), F32),
        scratch_shapes=[pltpu.VMEM((tm, d), F32)],
        compiler_params=_cparams(("parallel", "arbitrary")),
        name="merge",
    )(a_in, b_in, proj, proj, h, g_post, w_a, w_b, w_o)


def _ple_kernel(h_ref, p_ref, gpre_ref, gpost_ref, wg_ref, wp_ref, o_ref):
    h = h_ref[...]
    gate = jnp.dot(_rms(h, gpre_ref[...]).astype(BF16), wg_ref[...], preferred_element_type=F32)
    emb = jnp.dot(p_ref[...].astype(BF16), wp_ref[...], preferred_element_type=F32)
    o_ref[...] = h + _rms(emb * _sigmoid(gate), gpost_ref[...])


def _ple(h, p, g_pre, g_post, w_gate, w_proj):
    m, d = h.shape
    pd = p.shape[1]
    tm = _pick(m, (512, 256, 128))
    return pl.pallas_call(
        _ple_kernel,
        grid=(m // tm,),
        in_specs=[
            pl.BlockSpec((tm, d), lambda i: (i, 0)),
            pl.BlockSpec((tm, pd), lambda i: (i, 0)),
            pl.BlockSpec((1, d), lambda i: (0, 0)),
            pl.BlockSpec((1, d), lambda i: (0, 0)),
            pl.BlockSpec((d, d), lambda i: (0, 0)),
            pl.BlockSpec((pd, d), lambda i: (0, 0)),
        ],
        out_specs=pl.BlockSpec((tm, d), lambda i: (i, 0)),
        out_shape=jax.ShapeDtypeStruct((m, d), F32),
        compiler_params=_cparams(("parallel",)),
        name="ple",
    )(h, p, g_pre, g_post, w_gate, w_proj)


def _layer(x, p, conv_buf, h0, s0, reset_first, w):
    nb, t, d = x.shape
    m = nb * t
    xf = x.reshape(m, d)
    h1 = _ffn(xf, w["g_ffn1_pre"], w["g_ffn1_post"], w["w_ffn1_in"], w["w_ffn1_out"])
    proj = _proj(h1, w["g_mix_pre"], w["w_in"])
    cbuf = jnp.concatenate(
        [jnp.zeros((nb, SUBLANES - (CONV_W - 1), d), F32), conv_buf.astype(F32)], axis=1)
    a_in, h_last = _rglru(proj, nb, t, cbuf, h0.reshape(nb, 1, d), w["rg_conv_w"], w["rg_conv_b"],
                          w["rg_wax"], w["rg_bax"], w["rg_a_param"], reset_first)
    b_in, s_new = _hgrn2(proj, nb, t, s0, w["hg_lower_bound"], w["hg_norm_g"])
    h2 = _merge(a_in, b_in, proj, h1, w["g_mix_post"], w["w_branch_a"], w["w_branch_b"], w["w_out"])
    h3 = _ffn(h2, w["g_ffn2_pre"], w["g_ffn2_post"], w["w_ffn2_in"], w["w_ffn2_out"])
    y = _ple(h3, p.reshape(m, p.shape[-1]), w["g_ple_pre"], w["g_ple_post"], w["w_ple_gate"], w["w_ple_proj"])
    conv_new = proj.reshape(nb, t, -1)[:, t - (CONV_W - 1):, :d].astype(F32)
    return y.reshape(nb, t, d), conv_new, h_last.reshape(nb, d), s_new


def kernel(x_prompt, x_sample, p_prompt, p_sample, state_rglru_conv, state_rglru_h, state_hgrn2, g_ffn1_pre, g_ffn1_post, w_ffn1_in, w_ffn1_out, g_mix_pre, g_mix_post, w_in, rg_conv_w, rg_conv_b, rg_w_a, rg_b_a, rg_w_x, rg_b_x, rg_a_param, hg_lower_bound, hg_norm_g, w_branch_a, w_branch_b, w_out, g_ffn2_pre, g_ffn2_post, w_ffn2_in, w_ffn2_out, g_ple_pre, g_ple_post, w_ple_gate, w_ple_proj):
    assert g_ffn1_pre.shape[0] == 1 and hg_lower_bound.shape[0] == 2, "written for a single trunk layer"
    assert x_prompt.shape[1] >= CONV_W - 1 and x_sample.shape[1] >= CONV_W - 1
    l = 0
    bf = lambda a: a.astype(BF16)
    w = {
        "g_ffn1_pre": g_ffn1_pre[l][None], "g_ffn1_post": g_ffn1_post[l][None],
        "w_ffn1_in": bf(w_ffn1_in[l]), "w_ffn1_out": bf(w_ffn1_out[l]),
        "g_mix_pre": g_mix_pre[l][None], "g_mix_post": g_mix_post[l][None],
        "w_in": bf(w_in[l]),
        "rg_conv_w": rg_conv_w[l], "rg_conv_b": rg_conv_b[l][None],
        "rg_wax": bf(jnp.concatenate([rg_w_a[l], rg_w_x[l]], axis=-1)),
        "rg_bax": jnp.concatenate([rg_b_a[l], rg_b_x[l]], axis=-1)[:, None, :],
        "rg_a_param": rg_a_param[l][None],
        "hg_lower_bound": hg_lower_bound, "hg_norm_g": hg_norm_g[l][None],
        "w_branch_a": bf(w_branch_a[l]), "w_branch_b": bf(w_branch_b[l]), "w_out": bf(w_out[l]),
        "g_ffn2_pre": g_ffn2_pre[l][None], "g_ffn2_post": g_ffn2_post[l][None],
        "w_ffn2_in": bf(w_ffn2_in[l]), "w_ffn2_out": bf(w_ffn2_out[l]),
        "g_ple_pre": g_ple_pre[l][None], "g_ple_post": g_ple_post[l][None],
        "w_ple_gate": bf(w_ple_gate[l]), "w_ple_proj": bf(w_ple_proj[l]),
    }
    bp = x_prompt.shape[0]
    d = x_prompt.shape[-1]
    nh = state_hgrn2.shape[2]
    yp, conv_p, h_p, s_p = _layer(
        x_prompt, p_prompt[l],
        jnp.zeros((bp, CONV_W - 1, d), F32), jnp.zeros((bp, d), F32),
        jnp.zeros((bp, nh, HEAD, HEAD), F32), True, w)
    ys, conv_s, h_s, s_s = _layer(
        x_sample, p_sample[l], state_rglru_conv[l], state_rglru_h[l], state_hgrn2[l], False, w)
    return (yp, ys, conv_p[None], h_p[None], s_p[None], conv_s[None], h_s[None], s_s[None])
```

```python
import functools

import jax
import jax.numpy as jnp
import numpy as np
from jax import lax
from jax.experimental import pallas as pl
from jax.experimental.pallas import tpu as pltpu

F32 = jnp.float32
BF16 = jnp.bfloat16

EPS = 1e-6
RG_C = 8.0
CONV_W = 4
HEAD = 128
CHUNK = 64
SUBLANES = 8
VMEM_LIMIT = 56 * 1024 * 1024


def _cparams(sem, flags=None):
    return pltpu.CompilerParams(dimension_semantics=sem, vmem_limit_bytes=VMEM_LIMIT, flags=flags)


def _rms(x, g):
    return x * lax.rsqrt(jnp.mean(x * x, axis=-1, keepdims=True) + EPS) * g


def _sigmoid(x):
    return 0.5 + 0.5 * jnp.tanh(0.5 * x)


def _silu(x):
    h = 0.5 * x
    return h + h * jnp.tanh(h)


def _gelu_tanh(x):
    return 0.5 * x * (1.0 + jnp.tanh(0.7978845608028654 * (x + 0.044715 * (x * x * x))))


def _log1p(z):
    w = 1.0 + z
    return jnp.where(w == 1.0, z, jnp.log(w) * (z / jnp.where(w == 1.0, 1.0, w - 1.0)))


def _pick(n, prefs):
    for p in prefs:
        if n % p == 0:
            return p
    raise ValueError(f"no tile for {n} in {prefs}")


def _ffn_kernel(x_ref, gpre_ref, gpost_ref, wg_ref, wu_ref, wo_ref, o_ref, xn_ref, acc_ref):
    j = pl.program_id(1)

    @pl.when(j == 0)
    def _():
        xn_ref[...] = _rms(x_ref[...], gpre_ref[...]).astype(BF16)
        acc_ref[...] = jnp.zeros_like(acc_ref)

    xn = xn_ref[...]
    gate = jnp.dot(xn, wg_ref[...], preferred_element_type=F32)
    up = jnp.dot(xn, wu_ref[...], preferred_element_type=F32)
    act = (_silu(gate) * up).astype(BF16)
    acc_ref[...] += jnp.dot(act, wo_ref[...], preferred_element_type=F32)

    @pl.when(j == pl.num_programs(1) - 1)
    def _():
        o_ref[...] = x_ref[...] + 0.5 * _rms(acc_ref[...], gpost_ref[...])


def _ffn(x, g_pre, g_post, w_in, w_out):
    m, d = x.shape
    f = w_out.shape[0]
    tm = _pick(m, (512, 256, 128))
    tf = _pick(f, (512, 256, 128))
    nf = f // tf
    return pl.pallas_call(
        _ffn_kernel,
        grid=(m // tm, nf),
        in_specs=[
            pl.BlockSpec((tm, d), lambda i, j: (i, 0)),
            pl.BlockSpec((1, d), lambda i, j: (0, 0)),
            pl.BlockSpec((1, d), lambda i, j: (0, 0)),
            pl.BlockSpec((d, tf), lambda i, j: (0, j)),
            pl.BlockSpec((d, tf), lambda i, j: (0, j + nf)),
            pl.BlockSpec((tf, d), lambda i, j: (j, 0)),
        ],
        out_specs=pl.BlockSpec((tm, d), lambda i, j: (i, 0)),
        out_shape=jax.ShapeDtypeStruct((m, d), F32),
        scratch_shapes=[pltpu.VMEM((tm, d), BF16), pltpu.VMEM((tm, d), F32)],
        compiler_params=_cparams(("parallel", "arbitrary")),
        name="ffn",
    )(x, g_pre, g_post, w_in, w_in, w_out)


def _proj_kernel(x_ref, g_ref, w_ref, o_ref, xn_ref):
    @pl.when(pl.program_id(1) == 0)
    def _():
        xn_ref[...] = _rms(x_ref[...], g_ref[...]).astype(BF16)

    o_ref[...] = jnp.dot(xn_ref[...], w_ref[...], preferred_element_type=F32).astype(o_ref.dtype)


def _proj(x, g, w):
    m, d = x.shape
    n = w.shape[1]
    tm = _pick(m, (512, 256, 128))
    tn = _pick(n, (1024, 512, 256, 128))
    return pl.pallas_call(
        _proj_kernel,
        grid=(m // tm, n // tn),
        in_specs=[
            pl.BlockSpec((tm, d), lambda i, j: (i, 0)),
            pl.BlockSpec((1, d), lambda i, j: (0, 0)),
            pl.BlockSpec((d, tn), lambda i, j: (0, j)),
        ],
        out_specs=pl.BlockSpec((tm, tn), lambda i, j: (i, j)),
        out_shape=jax.ShapeDtypeStruct((m, n), BF16),
        scratch_shapes=[pltpu.VMEM((tm, d), BF16)],
        compiler_params=_cparams(("parallel", "arbitrary")),
        name="proj",
    )(x, g, w)


def _rglru_kernel(x_ref, y_ref, cbuf_ref, h0_ref, cw_ref, cb_ref, wax_ref, bax_ref, ap_ref,
                  o_ref, hlast_ref, xp_ref, a_ref, u_ref, carry_ref, *, tb, cols, reset_first):
    t = pl.program_id(2)
    pad = SUBLANES

    @pl.when(t == 0)
    def _():
        xp_ref[0:pad, :] = cbuf_ref[0]
        carry_ref[...] = h0_ref[0]

    xp_ref[pad:pad + tb, :] = x_ref[...].astype(F32)
    cw = cw_ref[...]
    xc = cb_ref[...] + cw[0:1, :] * xp_ref[pad:pad + tb, :]
    for j in range(1, CONV_W):
        xc = xc + cw[j:j + 1, :] * xp_ref[pad - j:pad - j + tb, :]
    xp_ref[0:pad, :] = xp_ref[tb:tb + pad, :]

    softplus_a = jnp.maximum(ap_ref[...], 0.0) + _log1p(jnp.exp(-jnp.abs(ap_ref[...])))
    if reset_first:
        row = lax.broadcasted_iota(jnp.int32, (tb, HEAD), 0)
        is_reset = jnp.logical_and(row == 0, t == 0)
    for h in range(cols // HEAD):
        sl = slice(h * HEAD, (h + 1) * HEAD)
        xh = xc[:, sl]
        gates = jnp.dot(xh.astype(BF16), wax_ref[h], preferred_element_type=F32) + bax_ref[h]
        r = _sigmoid(gates[:, :HEAD])
        i = _sigmoid(gates[:, HEAD:])
        log_a = (-RG_C) * r * softplus_a[:, sl]
        a = jnp.exp(log_a)
        mult = jnp.sqrt(1.0 - a * a)
        if reset_first:
            a = jnp.where(is_reset, 0.0, a)
            mult = jnp.where(is_reset, 1.0, mult)
        a_ref[:, sl] = a
        u_ref[:, sl] = mult * i * xh

    srow = lax.broadcasted_iota(jnp.int32, (SUBLANES, cols), 0)

    def body(g, carry):
        r0 = pl.multiple_of(g * SUBLANES, SUBLANES)
        a = a_ref[pl.ds(r0, SUBLANES), :]
        u = u_ref[pl.ds(r0, SUBLANES), :]
        for dd in (1, 2, 4):
            keep = srow >= dd
            u = jnp.where(keep, a * pltpu.roll(u, dd, 0) + u, u)
            a = jnp.where(keep, a * pltpu.roll(a, dd, 0), a)
        hcur = a * carry + u
        u_ref[pl.ds(r0, SUBLANES), :] = hcur
        return jnp.broadcast_to(hcur[SUBLANES - 1:SUBLANES, :], (SUBLANES, cols))

    carry = lax.fori_loop(0, tb // SUBLANES, body,
                          jnp.broadcast_to(carry_ref[...], (SUBLANES, cols)))
    carry_ref[...] = carry[0:1, :]
    hlast_ref[0] = carry[0:1, :]
    o_ref[...] = (_gelu_tanh(y_ref[...].astype(F32)) * u_ref[...]).astype(o_ref.dtype)


def _rglru(proj, nb, t, conv_buf, h0, cw, cb, wax, bax, a_param, reset_first):
    d = cw.shape[1]
    cols = 512
    hpc = cols // HEAD
    tb = _pick(t, (512, 256, 128, 64))
    nt = t // tb
    ncol = d // cols
    kern = functools.partial(_rglru_kernel, tb=tb, cols=cols, reset_first=reset_first)
    return pl.pallas_call(
        kern,
        grid=(nb, ncol, nt),
        in_specs=[
            pl.BlockSpec((tb, cols), lambda b, c, s: (b * nt + s, c)),
            pl.BlockSpec((tb, cols), lambda b, c, s: (b * nt + s, ncol + c)),
            pl.BlockSpec((1, SUBLANES, cols), lambda b, c, s: (b, 0, c)),
            pl.BlockSpec((1, 1, cols), lambda b, c, s: (b, 0, c)),
            pl.BlockSpec((CONV_W, cols), lambda b, c, s: (0, c)),
            pl.BlockSpec((1, cols), lambda b, c, s: (0, c)),
            pl.BlockSpec((hpc, HEAD, 2 * HEAD), lambda b, c, s: (c, 0, 0)),
            pl.BlockSpec((hpc, 1, 2 * HEAD), lambda b, c, s: (c, 0, 0)),
            pl.BlockSpec((1, cols), lambda b, c, s: (0, c)),
        ],
        out_specs=[
            pl.BlockSpec((tb, cols), lambda b, c, s: (b * nt + s, c)),
            pl.BlockSpec((1, 1, cols), lambda b, c, s: (b, 0, c)),
        ],
        out_shape=[
            jax.ShapeDtypeStruct((nb * t, d), BF16),
            jax.ShapeDtypeStruct((nb, 1, d), F32),
        ],
        scratch_shapes=[
            pltpu.VMEM((tb + SUBLANES, cols), F32),
            pltpu.VMEM((tb, cols), F32),
            pltpu.VMEM((tb, cols), F32),
            pltpu.VMEM((1, cols), F32),
        ],
        compiler_params=_cparams(("parallel", "parallel", "arbitrary")),
        name="rglru",
    )(proj, proj, conv_buf, h0, cw, cb, wax, bax, a_param)


LEVELS = (32, 16, 8, 4, 2, 1)


def _hgrn2_masks():
    t = np.arange(CHUNK)[:, None]
    s = np.arange(CHUNK)[None, :]
    out = []
    for half in LEVELS:
        same = (t // (2 * half)) == (s // (2 * half))
        out.append(same & (t % (2 * half) >= half) & (s % (2 * half) < half))
    out.append(t == s)
    return jnp.asarray(np.stack(out), F32)


def _row_bcast(b, first, step):
    span = max(step, SUBLANES)
    return jnp.concatenate(
        [jnp.broadcast_to(b[r:r + 1, :], (span, b.shape[1])) for r in range(first, CHUNK, span)], axis=0)


def _level_ref_rows(b, half, row):
    if 2 * half >= SUBLANES:
        return _row_bcast(b, half - 1, 2 * half)
    if half == 2:
        return jnp.where(jnp.bitwise_and(row, 4) == 0, _row_bcast(b, 1, SUBLANES), _row_bcast(b, 5, SUBLANES))
    assert half == 1
    return jnp.where(jnp.bitwise_and(row, 1) == 0, b, pltpu.roll(b, 1, 0))


def _hgrn2_kernel(q_ref, f_ref, v_ref, g_ref, s0_ref, lbp_ref, gn_ref, lvl_ref,
                  o_ref, snew_ref, st_ref, *, tb, hpb):
    t = pl.program_id(2)
    width = hpb * HEAD

    @pl.when(t == 0)
    def _():
        for h in range(hpb):
            st_ref[h] = s0_ref[0, h].T

    lbp = lbp_ref[...]
    le = jnp.exp(lbp - jnp.max(lbp, axis=0, keepdims=True))
    lb = le[0:1, :] / jnp.sum(le, axis=0, keepdims=True)
    f_mid = 0.5 * (1.0 + lb)
    f_amp = 0.5 * (1.0 - lb)
    gn = gn_ref[...]
    row = lax.broadcasted_iota(jnp.int32, (CHUNK, width), 0)
    his = [jnp.bitwise_and(row, 2 * half - 1) >= half for half in LEVELS]
    tri = (lax.broadcasted_iota(jnp.int32, (CHUNK, CHUNK), 0)
           >= lax.broadcasted_iota(jnp.int32, (CHUNK, CHUNK), 1)).astype(BF16)
    nt_dims = (((1,), (1,)), ((), ()))
    heads = [slice(h * HEAD, (h + 1) * HEAD) for h in range(hpb)]

    def chunk(c, carry):
        r0 = pl.multiple_of(c * CHUNK, CHUNK)
        qp = q_ref[pl.ds(r0, CHUNK), :].astype(F32)
        fp = f_ref[pl.ds(r0, CHUNK), :].astype(F32)
        v = v_ref[pl.ds(r0, CHUNK), :]
        gp = g_ref[pl.ds(r0, CHUNK), :].astype(F32)
        sts = [st_ref[h] for h in range(hpb)]
        f = f_mid + f_amp * jnp.tanh(0.5 * fp)
        k = 1.0 - f
        q = _silu(qp)
        lf = jnp.log(f)
        lf_hi = lf.astype(BF16)
        lf_lo = (lf - lf_hi.astype(F32)).astype(BF16)
        b2 = jnp.dot(tri, jnp.concatenate([lf_hi, lf_lo], axis=1), preferred_element_type=F32)
        b = b2[:, :width] + b2[:, width:]
        qe = (q * jnp.exp(b)).astype(BF16)
        outs = [lax.dot_general(qe[:, sl], sts[h].astype(BF16), nt_dims, preferred_element_type=F32)
                for h, sl in enumerate(heads)]
        scores = [None] * hpb
        for i, half in enumerate(LEVELS):
            e = jnp.exp(-jnp.abs(b - _level_ref_rows(b, half, row)))
            z = (jnp.where(his[i], q, k) * e).astype(BF16)
            mask = lvl_ref[i]
            for h, sl in enumerate(heads):
                term = lax.dot_general(z[:, sl], z[:, sl], nt_dims, preferred_element_type=F32) * mask
                scores[h] = term if scores[h] is None else scores[h] + term
        q16 = q.astype(BF16)
        k16 = k.astype(BF16)
        eye = lvl_ref[len(LEVELS)]
        blast = b[CHUNK - 1:CHUNK, :]
        kd = (k * jnp.exp(blast - b)).astype(BF16)
        eblast = jnp.exp(blast)
        for h, sl in enumerate(heads):
            a_full = scores[h] + eye * lax.dot_general(q16[:, sl], k16[:, sl], nt_dims,
                                                       preferred_element_type=F32)
            outs[h] = outs[h] + jnp.dot(a_full.astype(BF16), v[:, sl], preferred_element_type=F32)
            st_ref[h] = sts[h] * eblast[:, sl] + jnp.dot(v[:, sl].astype(F32).T.astype(BF16), kd[:, sl],
                                                          preferred_element_type=F32)
        o = jnp.concatenate([_rms(outs[h], gn[:, sl]) for h, sl in enumerate(heads)], axis=1)
        o_ref[pl.ds(r0, CHUNK), :] = (o * _silu(gp)).astype(o_ref.dtype)
        return carry

    lax.fori_loop(0, tb // CHUNK, chunk, 0)
    for h in range(hpb):
        snew_ref[0, h] = st_ref[h].T


def _hgrn2(proj, nb, t, s0, lb_param, gn):
    nh = s0.shape[1]
    d = nh * HEAD
    hpb = 8
    cols = hpb * HEAD
    ng = nh // hpb
    tb = _pick(t, (1024, 512, 256, 128, 64))
    nt = t // tb
    kern = functools.partial(_hgrn2_kernel, tb=tb, hpb=hpb)
    lvl = _hgrn2_masks()

    def col_spec(group):
        return pl.BlockSpec((tb, cols), lambda b, h, s: (b * nt + s, group * ng + h))

    return pl.pallas_call(
        kern,
        grid=(nb, ng, nt),
        in_specs=[
            col_spec(2), col_spec(3), col_spec(4), col_spec(5),
            pl.BlockSpec((1, hpb, HEAD, HEAD), lambda b, h, s: (b, h, 0, 0)),
            pl.BlockSpec((lb_param.shape[0], cols), lambda b, h, s: (0, h)),
            pl.BlockSpec((1, cols), lambda b, h, s: (0, h)),
            pl.BlockSpec(lvl.shape, lambda b, h, s: (0, 0, 0)),
        ],
        out_specs=[
            pl.BlockSpec((tb, cols), lambda b, h, s: (b * nt + s, h)),
            pl.BlockSpec((1, hpb, HEAD, HEAD), lambda b, h, s: (b, h, 0, 0)),
        ],
        out_shape=[
            jax.ShapeDtypeStruct((nb * t, d), BF16),
            jax.ShapeDtypeStruct(s0.shape, F32),
        ],
        scratch_shapes=[pltpu.VMEM((hpb, HEAD, HEAD), F32)],
        compiler_params=_cparams(("parallel", "parallel", "arbitrary")),
        name="hgrn2",
    )(proj, proj, proj, proj, s0, lb_param, gn, lvl)


def _merge_kernel(a_ref, b_ref, ga_ref, gb_ref, h_ref, g_ref, wa_ref, wb_ref, wo_ref, o_ref, acc_ref):
    j = pl.program_id(1)

    @pl.when(j == 0)
    def _():
        acc_ref[...] = jnp.zeros_like(acc_ref)

    br_a = jnp.dot(a_ref[...], wa_ref[...], preferred_element_type=F32)
    br_b = jnp.dot(b_ref[...], wb_ref[...], preferred_element_type=F32)
    m = _sigmoid(ga_ref[...].astype(F32)) * br_a + _sigmoid(gb_ref[...].astype(F32)) * br_b
    acc_ref[...] += jnp.dot(m.astype(BF16), wo_ref[...], preferred_element_type=F32)

    @pl.when(j == pl.num_programs(1) - 1)
    def _():
        o_ref[...] = h_ref[...] + _rms(acc_ref[...], g_ref[...])


def _merge(a_in, b_in, proj, h, g_post, w_a, w_b, w_o):
    m, d = h.shape
    tm = _pick(m, (512, 256, 128))
    tn = 512
    nn = d // tn
    return pl.pallas_call(
        _merge_kernel,
        grid=(m // tm, nn),
        in_specs=[
            pl.BlockSpec((tm, d), lambda i, j: (i, 0)),
            pl.BlockSpec((tm, d), lambda i, j: (i, 0)),
            pl.BlockSpec((tm, tn), lambda i, j: (i, 6 * nn + j)),
            pl.BlockSpec((tm, tn), lambda i, j: (i, 7 * nn + j)),
            pl.BlockSpec((tm, d), lambda i, j: (i, 0)),
            pl.BlockSpec((1, d), lambda i, j: (0, 0)),
            pl.BlockSpec((d, tn), lambda i, j: (0, j)),
            pl.BlockSpec((d, tn), lambda i, j: (0, j)),
            pl.BlockSpec((tn, d), lambda i, j: (j, 0)),
        ],
        out_specs=pl.BlockSpec((tm, d), lambda i, j: (i, 0)),
        out_shape=jax.ShapeDtypeStruct((m, d), F32),
        scratch_shapes=[pltpu.VMEM((tm, d), F32)],
        compiler_params=_cparams(("parallel", "arbitrary")),
        name="merge",
    )(a_in, b_in, proj, proj, h, g_post, w_a, w_b, w_o)


def _ple_kernel(h_ref, p_ref, gpre_ref, gpost_ref, wg_ref, wp_ref, o_ref):
    h = h_ref[...]
    gate = jnp.dot(_rms(h, gpre_ref[...]).astype(BF16), wg_ref[...], preferred_element_type=F32)
    emb = jnp.dot(p_ref[...].astype(BF16), wp_ref[...], preferred_element_type=F32)
    o_ref[...] = h + _rms(emb * _sigmoid(gate), gpost_ref[...])


def _ple(h, p, g_pre, g_post, w_gate, w_proj):
    m, d = h.shape
    pd = p.shape[1]
    tm = _pick(m, (512, 256, 128))
    return pl.pallas_call(
        _ple_kernel,
        grid=(m // tm,),
        in_specs=[
            pl.BlockSpec((tm, d), lambda i: (i, 0)),
            pl.BlockSpec((tm, pd), lambda i: (i, 0)),
            pl.BlockSpec((1, d), lambda i: (0, 0)),
            pl.BlockSpec((1, d), lambda i: (0, 0)),
            pl.BlockSpec((d, d), lambda i: (0, 0)),
            pl.BlockSpec((pd, d), lambda i: (0, 0)),
        ],
        out_specs=pl.BlockSpec((tm, d), lambda i: (i, 0)),
        out_shape=jax.ShapeDtypeStruct((m, d), F32),
        compiler_params=_cparams(("parallel",)),
        name="ple",
    )(h, p, g_pre, g_post, w_gate, w_proj)


def _layer(x, p, conv_buf, h0, s0, reset_first, w):
    nb, t, d = x.shape
    m = nb * t
    xf = x.reshape(m, d)
    h1 = _ffn(xf, w["g_ffn1_pre"], w["g_ffn1_post"], w["w_ffn1_in"], w["w_ffn1_out"])
    proj = _proj(h1, w["g_mix_pre"], w["w_in"])
    cbuf = jnp.concatenate(
        [jnp.zeros((nb, SUBLANES - (CONV_W - 1), d), F32), conv_buf.astype(F32)], axis=1)
    a_in, h_last = _rglru(proj, nb, t, cbuf, h0.reshape(nb, 1, d), w["rg_conv_w"], w["rg_conv_b"],
                          w["rg_wax"], w["rg_bax"], w["rg_a_param"], reset_first)
    b_in, s_new = _hgrn2(proj, nb, t, s0, w["hg_lower_bound"], w["hg_norm_g"])
    h2 = _merge(a_in, b_in, proj, h1, w["g_mix_post"], w["w_branch_a"], w["w_branch_b"], w["w_out"])
    h3 = _ffn(h2, w["g_ffn2_pre"], w["g_ffn2_post"], w["w_ffn2_in"], w["w_ffn2_out"])
    y = _ple(h3, p.reshape(m, p.shape[-1]), w["g_ple_pre"], w["g_ple_post"], w["w_ple_gate"], w["w_ple_proj"])
    conv_new = proj.reshape(nb, t, -1)[:, t - (CONV_W - 1):, :d].astype(F32)
    return y.reshape(nb, t, d), conv_new, h_last.reshape(nb, d), s_new


def kernel(x_prompt, x_sample, p_prompt, p_sample, state_rglru_conv, state_rglru_h, state_hgrn2, g_ffn1_pre, g_ffn1_post, w_ffn1_in, w_ffn1_out, g_mix_pre, g_mix_post, w_in, rg_conv_w, rg_conv_b, rg_w_a, rg_b_a, rg_w_x, rg_b_x, rg_a_param, hg_lower_bound, hg_norm_g, w_branch_a, w_branch_b, w_out, g_ffn2_pre, g_ffn2_post, w_ffn2_in, w_ffn2_out, g_ple_pre, g_ple_post, w_ple_gate, w_ple_proj):
    assert g_ffn1_pre.shape[0] == 1 and hg_lower_bound.shape[0] == 2, "written for a single trunk layer"
    assert x_prompt.shape[1] >= CONV_W - 1 and x_sample.shape[1] >= CONV_W - 1
    l = 0
    bf = lambda a: a.astype(BF16)
    w = {
        "g_ffn1_pre": g_ffn1_pre[l][None], "g_ffn1_post": g_ffn1_post[l][None],
        "w_ffn1_in": bf(w_ffn1_in[l]), "w_ffn1_out": bf(w_ffn1_out[l]),
        "g_mix_pre": g_mix_pre[l][None], "g_mix_post": g_mix_post[l][None],
        "w_in": bf(w_in[l]),
        "rg_conv_w": rg_conv_w[l], "rg_conv_b": rg_conv_b[l][None],
        "rg_wax": bf(jnp.concatenate([rg_w_a[l], rg_w_x[l]], axis=-1)),
        "rg_bax": jnp.concatenate([rg_b_a[l], rg_b_x[l]], axis=-1)[:, None, :],
        "rg_a_param": rg_a_param[l][None],
        "hg_lower_bound": hg_lower_bound, "hg_norm_g": hg_norm_g[l][None],
        "w_branch_a": bf(w_branch_a[l]), "w_branch_b": bf(w_branch_b[l]), "w_out": bf(w_out[l]),
        "g_ffn2_pre": g_ffn2_pre[l][None], "g_ffn2_post": g_ffn2_post[l][None],
        "w_ffn2_in": bf(w_ffn2_in[l]), "w_ffn2_out": bf(w_ffn2_out[l]),
        "g_ple_pre": g_ple_pre[l][None], "g_ple_post": g_ple_post[l][None],
        "w_ple_gate": bf(w_ple_gate[l]), "w_ple_proj": bf(w_ple_proj[l]),
    }
    bp = x_prompt.shape[0]
    d = x_prompt.shape[-1]
    nh = state_hgrn2.shape[2]
    yp, conv_p, h_p, s_p = _layer(
        x_prompt, p_prompt[l],
        jnp.zeros((bp, CONV_W - 1, d), F32), jnp.zeros((bp, d), F32),
        jnp.zeros((bp, nh, HEAD, HEAD), F32), True, w)
    ys, conv_s, h_s, s_s = _layer(
        x_sample, p_sample[l], state_rglru_conv[l], state_rglru_h[l], state_hgrn2[l], False, w)
    return (yp, ys, conv_p[None], h_p[None], s_p[None], conv_s[None], h_s[None], s_s[None])
```

```python
import functools

import jax
import jax.numpy as jnp
import numpy as np
from jax import lax
from jax.experimental import pallas as pl
from jax.experimental.pallas import tpu as pltpu

F32 = jnp.float32
BF16 = jnp.bfloat16

EPS = 1e-6
RG_C = 8.0
CONV_W = 4
HEAD = 128
CHUNK = 64
SUBLANES = 8
VMEM_LIMIT = 56 * 1024 * 1024
LOG2E = 1.4426950408889634


def _cparams(sem, flags=None):
    return pltpu.CompilerParams(dimension_semantics=sem, vmem_limit_bytes=VMEM_LIMIT, flags=flags)


def _rms(x, g):
    return x * lax.rsqrt(jnp.mean(x * x, axis=-1, keepdims=True) + EPS) * g


def _sigmoid(x):
    return 0.5 + 0.5 * jnp.tanh(0.5 * x)


def _silu(x):
    h = 0.5 * x
    return h + h * jnp.tanh(h)


def _gelu_tanh(x):
    return 0.5 * x * (1.0 + jnp.tanh(0.7978845608028654 * (x + 0.044715 * (x * x * x))))


def _log1p(z):
    w = 1.0 + z
    return jnp.where(w == 1.0, z, jnp.log(w) * (z / jnp.where(w == 1.0, 1.0, w - 1.0)))


def _pick(n, prefs):
    for p in prefs:
        if n % p == 0:
            return p
    raise ValueError(f"no tile for {n} in {prefs}")


def _ffn_kernel(x_ref, gpre_ref, gpost_ref, wg_ref, wu_ref, wo_ref, o_ref, xn_ref, acc_ref):
    j = pl.program_id(1)

    @pl.when(j == 0)
    def _():
        xn_ref[...] = _rms(x_ref[...], gpre_ref[...]).astype(BF16)
        acc_ref[...] = jnp.zeros_like(acc_ref)

    xn = xn_ref[...]
    gate = jnp.dot(xn, wg_ref[...], preferred_element_type=F32)
    up = jnp.dot(xn, wu_ref[...], preferred_element_type=F32)
    act = (_silu(gate) * up).astype(BF16)
    acc_ref[...] += jnp.dot(act, wo_ref[...], preferred_element_type=F32)

    @pl.when(j == pl.num_programs(1) - 1)
    def _():
        o_ref[...] = x_ref[...] + 0.5 * _rms(acc_ref[...], gpost_ref[...])


def _ffn(x, g_pre, g_post, w_in, w_out):
    m, d = x.shape
    f = w_out.shape[0]
    tm = _pick(m, (512, 256, 128))
    tf = _pick(f, (512, 256, 128))
    nf = f // tf
    return pl.pallas_call(
        _ffn_kernel,
        grid=(m // tm, nf),
        in_specs=[
            pl.BlockSpec((tm, d), lambda i, j: (i, 0)),
            pl.BlockSpec((1, d), lambda i, j: (0, 0)),
            pl.BlockSpec((1, d), lambda i, j: (0, 0)),
            pl.BlockSpec((d, tf), lambda i, j: (0, j)),
            pl.BlockSpec((d, tf), lambda i, j: (0, j + nf)),
            pl.BlockSpec((tf, d), lambda i, j: (j, 0)),
        ],
        out_specs=pl.BlockSpec((tm, d), lambda i, j: (i, 0)),
        out_shape=jax.ShapeDtypeStruct((m, d), F32),
        scratch_shapes=[pltpu.VMEM((tm, d), BF16), pltpu.VMEM((tm, d), F32)],
        compiler_params=_cparams(("parallel", "arbitrary")),
        name="ffn",
    )(x, g_pre, g_post, w_in, w_in, w_out)


def _proj_kernel(x_ref, g_ref, w_ref, o_ref, xn_ref):
    @pl.when(pl.program_id(1) == 0)
    def _():
        xn_ref[...] = _rms(x_ref[...], g_ref[...]).astype(BF16)

    o_ref[...] = jnp.dot(xn_ref[...], w_ref[...], preferred_element_type=F32).astype(o_ref.dtype)


def _proj(x, g, w):
    m, d = x.shape
    n = w.shape[1]
    tm = _pick(m, (1024, 512, 256, 128))
    tn = _pick(n, (1024, 512, 256, 128))
    return pl.pallas_call(
        _proj_kernel,
        grid=(m // tm, n // tn),
        in_specs=[
            pl.BlockSpec((tm, d), lambda i, j: (i, 0)),
            pl.BlockSpec((1, d), lambda i, j: (0, 0)),
            pl.BlockSpec((d, tn), lambda i, j: (0, j)),
        ],
        out_specs=pl.BlockSpec((tm, tn), lambda i, j: (i, j)),
        out_shape=jax.ShapeDtypeStruct((m, n), BF16),
        scratch_shapes=[pltpu.VMEM((tm, d), BF16)],
        compiler_params=_cparams(("parallel", "arbitrary")),
        name="proj",
    )(x, g, w)


def _rglru_kernel(x_ref, y_ref, cbuf_ref, h0_ref, cw_ref, cb_ref, wax_ref, bax_ref, ap_ref,
                  o_ref, hlast_ref, xp_ref, a_ref, u_ref, carry_ref, *, tb, cols, reset_first):
    t = pl.program_id(2)
    pad = SUBLANES

    @pl.when(t == 0)
    def _():
        xp_ref[0:pad, :] = cbuf_ref[0]
        carry_ref[...] = h0_ref[0]

    xp_ref[pad:pad + tb, :] = x_ref[...].astype(F32)
    cw = cw_ref[...]
    xc = cb_ref[...] + cw[0:1, :] * xp_ref[pad:pad + tb, :]
    for j in range(1, CONV_W):
        xc = xc + cw[j:j + 1, :] * xp_ref[pad - j:pad - j + tb, :]
    xp_ref[0:pad, :] = xp_ref[tb:tb + pad, :]

    softplus_a = jnp.maximum(ap_ref[...], 0.0) + _log1p(jnp.exp(-jnp.abs(ap_ref[...])))
    if reset_first:
        row = lax.broadcasted_iota(jnp.int32, (tb, HEAD), 0)
        is_reset = jnp.logical_and(row == 0, t == 0)
    for h in range(cols // HEAD):
        sl = slice(h * HEAD, (h + 1) * HEAD)
        xh = xc[:, sl]
        gates = jnp.dot(xh.astype(BF16), wax_ref[h], preferred_element_type=F32) + bax_ref[h]
        r = _sigmoid(gates[:, :HEAD])
        i = _sigmoid(gates[:, HEAD:])
        log_a = (-RG_C) * r * softplus_a[:, sl]
        a = jnp.exp(log_a)
        mult = jnp.sqrt(1.0 - a * a)
        if reset_first:
            a = jnp.where(is_reset, 0.0, a)
            mult = jnp.where(is_reset, 1.0, mult)
        a_ref[:, sl] = a
        u_ref[:, sl] = mult * i * xh

    srow = lax.broadcasted_iota(jnp.int32, (SUBLANES, cols), 0)

    def body(g, carry):
        r0 = pl.multiple_of(g * SUBLANES, SUBLANES)
        a = a_ref[pl.ds(r0, SUBLANES), :]
        u = u_ref[pl.ds(r0, SUBLANES), :]
        for dd in (1, 2, 4):
            keep = srow >= dd
            u = jnp.where(keep, a * pltpu.roll(u, dd, 0) + u, u)
            a = jnp.where(keep, a * pltpu.roll(a, dd, 0), a)
        hcur = a * carry + u
        u_ref[pl.ds(r0, SUBLANES), :] = hcur
        return jnp.broadcast_to(hcur[SUBLANES - 1:SUBLANES, :], (SUBLANES, cols))

    carry = lax.fori_loop(0, tb // SUBLANES, body,
                          jnp.broadcast_to(carry_ref[...], (SUBLANES, cols)))
    carry_ref[...] = carry[0:1, :]
    hlast_ref[0] = carry[0:1, :]
    o_ref[...] = (_gelu_tanh(y_ref[...].astype(F32)) * u_ref[...]).astype(o_ref.dtype)


def _rglru(proj, nb, t, conv_buf, h0, cw, cb, wax, bax, a_param, reset_first):
    d = cw.shape[1]
    cols = 512
    hpc = cols // HEAD
    tb = _pick(t, (512, 256, 128, 64))
    nt = t // tb
    ncol = d // cols
    kern = functools.partial(_rglru_kernel, tb=tb, cols=cols, reset_first=reset_first)
    return pl.pallas_call(
        kern,
        grid=(nb, ncol, nt),
        in_specs=[
            pl.BlockSpec((tb, cols), lambda b, c, s: (b * nt + s, c)),
            pl.BlockSpec((tb, cols), lambda b, c, s: (b * nt + s, ncol + c)),
            pl.BlockSpec((1, SUBLANES, cols), lambda b, c, s: (b, 0, c)),
            pl.BlockSpec((1, 1, cols), lambda b, c, s: (b, 0, c)),
            pl.BlockSpec((CONV_W, cols), lambda b, c, s: (0, c)),
            pl.BlockSpec((1, cols), lambda b, c, s: (0, c)),
            pl.BlockSpec((hpc, HEAD, 2 * HEAD), lambda b, c, s: (c, 0, 0)),
            pl.BlockSpec((hpc, 1, 2 * HEAD), lambda b, c, s: (c, 0, 0)),
            pl.BlockSpec((1, cols), lambda b, c, s: (0, c)),
        ],
        out_specs=[
            pl.BlockSpec((tb, cols), lambda b, c, s: (b * nt + s, c)),
            pl.BlockSpec((1, 1, cols), lambda b, c, s: (b, 0, c)),
        ],
        out_shape=[
            jax.ShapeDtypeStruct((nb * t, d), BF16),
            jax.ShapeDtypeStruct((nb, 1, d), F32),
        ],
        scratch_shapes=[
            pltpu.VMEM((tb + SUBLANES, cols), F32),
            pltpu.VMEM((tb, cols), F32),
            pltpu.VMEM((tb, cols), F32),
            pltpu.VMEM((1, cols), F32),
        ],
        compiler_params=_cparams(("parallel", "parallel", "arbitrary")),
        name="rglru",
    )(proj, proj, conv_buf, h0, cw, cb, wax, bax, a_param)


LEVELS = (32, 16, 8, 4, 2, 1)


def _hgrn2_masks():
    t = np.arange(CHUNK)[:, None]
    s = np.arange(CHUNK)[None, :]
    out = []
    for half in LEVELS:
        same = (t // (2 * half)) == (s // (2 * half))
        out.append(same & (t % (2 * half) >= half) & (s % (2 * half) < half))
    out.append(t == s)
    return jnp.asarray(np.stack(out), F32)


def _row_bcast(b, first, step):
    span = max(step, SUBLANES)
    return jnp.concatenate(
        [jnp.broadcast_to(b[r:r + 1, :], (span, b.shape[1])) for r in range(first, b.shape[0], span)], axis=0)


def _level_ref_rows(b, half, row):
    if 2 * half >= SUBLANES:
        return _row_bcast(b, half - 1, 2 * half)
    if half == 2:
        return jnp.where(jnp.bitwise_and(row, 4) == 0, _row_bcast(b, 1, SUBLANES), _row_bcast(b, 5, SUBLANES))
    assert half == 1
    return jnp.where(jnp.bitwise_and(row, 1) == 0, b, pltpu.roll(b, 1, 0))


def _hgrn2_kernel(q_ref, f_ref, v_ref, g_ref, s0_ref, lbp_ref, gn_ref, lvl_ref,
                  o_ref, snew_ref, st_ref, *, tb, hpb, per_trip):
    t = pl.program_id(2)
    width = hpb * HEAD
    rows = per_trip * CHUNK

    @pl.when(t == 0)
    def _():
        for h in range(hpb):
            st_ref[h] = s0_ref[0, h].T

    lbp = lbp_ref[...]
    le = jnp.exp(lbp - jnp.max(lbp, axis=0, keepdims=True))
    lb = le[0:1, :] / jnp.sum(le, axis=0, keepdims=True)
    f_mid = 0.5 * (1.0 + lb)
    f_amp = 0.5 * (1.0 - lb)
    gn = gn_ref[...]
    row = lax.broadcasted_iota(jnp.int32, (rows, width), 0)
    his = [jnp.bitwise_and(row, 2 * half - 1) >= half for half in LEVELS]
    row16 = row.astype(jnp.int16)
    his16 = [jnp.bitwise_and(row16, 2 * half - 1) >= half for half in LEVELS]
    tr = lax.broadcasted_iota(jnp.int32, (rows, rows), 0)
    tc = lax.broadcasted_iota(jnp.int32, (rows, rows), 1)
    tri = jnp.logical_and(tr >= tc, tr // CHUNK == tc // CHUNK).astype(BF16)
    nt_dims = (((1,), (1,)), ((), ()))
    heads = [slice(h * HEAD, (h + 1) * HEAD) for h in range(hpb)]
    chunks = [slice(u * CHUNK, (u + 1) * CHUNK) for u in range(per_trip)]

    def body(c, carry):
        r0 = pl.multiple_of(c * rows, rows)
        qp = q_ref[pl.ds(r0, rows), :].astype(F32)
        fp = f_ref[pl.ds(r0, rows), :].astype(F32)
        v = v_ref[pl.ds(r0, rows), :]
        gp = g_ref[pl.ds(r0, rows), :].astype(F32)
        sts = [st_ref[h] for h in range(hpb)]
        f = f_mid + f_amp * jnp.tanh(0.5 * fp)
        k = 1.0 - f
        q = _silu(qp)
        lf = jnp.log(f)
        lf_hi = lf.astype(BF16)
        lf_lo = (lf - lf_hi.astype(F32)).astype(BF16)
        b2 = jnp.dot(tri, jnp.concatenate([lf_hi, lf_lo], axis=1), preferred_element_type=F32)
        b = b2[:, :width] + b2[:, width:]
        qe = (q * jnp.exp(b)).astype(BF16)
        q16 = q.astype(BF16)
        k16 = k.astype(BF16)
        eye = lvl_ref[len(LEVELS)]
        scores = [[eye * lax.dot_general(q16[cs, sl], k16[cs, sl], nt_dims, preferred_element_type=F32)
                   for sl in heads] for cs in chunks]
        for i, half in enumerate(LEVELS):
            e = jnp.exp2((b - _level_ref_rows(b, half, row)) * jnp.where(his[i], LOG2E, -LOG2E))
            z = jnp.where(his16[i], q16, k16) * e.astype(BF16)
            mask = lvl_ref[i]
            for u, cs in enumerate(chunks):
                for h, sl in enumerate(heads):
                    gram = lax.dot_general(z[cs, sl], z[cs, sl], nt_dims, preferred_element_type=F32)
                    scores[u][h] = scores[u][h] + gram * mask
        blast = _row_bcast(b, CHUNK - 1, CHUNK)
        kd = (k * jnp.exp(blast - b)).astype(BF16)
        eblast = jnp.exp(blast)
        vt = [[v[cs, sl].astype(F32).T.astype(BF16) for sl in heads] for cs in chunks]
        out_rows = []
        for u, cs in enumerate(chunks):
            outs = []
            for h, sl in enumerate(heads):
                o = lax.dot_general(qe[cs, sl], sts[h].astype(BF16), nt_dims, preferred_element_type=F32)
                o = o + jnp.dot(scores[u][h].astype(BF16), v[cs, sl], preferred_element_type=F32)
                outs.append(_rms(o, gn[:, sl]))
                sts[h] = (sts[h] * eblast[u * CHUNK:u * CHUNK + 1, sl]
                          + jnp.dot(vt[u][h], kd[cs, sl], preferred_element_type=F32))
            out_rows.append(jnp.concatenate(outs, axis=1))
        o_all = jnp.concatenate(out_rows, axis=0) if per_trip > 1 else out_rows[0]
        o_ref[pl.ds(r0, rows), :] = (o_all * _silu(gp)).astype(o_ref.dtype)
        for h in range(hpb):
            st_ref[h] = sts[h]
        return carry

    lax.fori_loop(0, tb // rows, body, 0)
    for h in range(hpb):
        snew_ref[0, h] = st_ref[h].T


def _hgrn2(proj, nb, t, s0, lb_param, gn):
    nh = s0.shape[1]
    d = nh * HEAD
    hpb = 8
    cols = hpb * HEAD
    ng = nh // hpb
    tb = _pick(t, (1024, 512, 256, 128, 64))
    nt = t // tb
    per_trip = _pick(tb // CHUNK, (4, 2, 1))
    kern = functools.partial(_hgrn2_kernel, tb=tb, hpb=hpb, per_trip=per_trip)
    lvl = _hgrn2_masks()

    def col_spec(group):
        return pl.BlockSpec((tb, cols), lambda b, h, s: (b * nt + s, group * ng + h))

    return pl.pallas_call(
        kern,
        grid=(nb, ng, nt),
        in_specs=[
            col_spec(2), col_spec(3), col_spec(4), col_spec(5),
            pl.BlockSpec((1, hpb, HEAD, HEAD), lambda b, h, s: (b, h, 0, 0)),
            pl.BlockSpec((lb_param.shape[0], cols), lambda b, h, s: (0, h)),
            pl.BlockSpec((1, cols), lambda b, h, s: (0, h)),
            pl.BlockSpec(lvl.shape, lambda b, h, s: (0, 0, 0)),
        ],
        out_specs=[
            pl.BlockSpec((tb, cols), lambda b, h, s: (b * nt + s, h)),
            pl.BlockSpec((1, hpb, HEAD, HEAD), lambda b, h, s: (b, h, 0, 0)),
        ],
        out_shape=[
            jax.ShapeDtypeStruct((nb * t, d), BF16),
            jax.ShapeDtypeStruct(s0.shape, F32),
        ],
        scratch_shapes=[pltpu.VMEM((hpb, HEAD, HEAD), F32)],
        compiler_params=_cparams(("parallel", "parallel", "arbitrary")),
        name="hgrn2",
    )(proj, proj, proj, proj, s0, lb_param, gn, lvl)


def _merge_kernel(a_ref, b_ref, ga_ref, gb_ref, h_ref, g_ref, wa_ref, wb_ref, wo_ref, o_ref, acc_ref):
    j = pl.program_id(1)

    @pl.when(j == 0)
    def _():
        acc_ref[...] = jnp.zeros_like(acc_ref)

    br_a = jnp.dot(a_ref[...], wa_ref[...], preferred_element_type=F32)
    br_b = jnp.dot(b_ref[...], wb_ref[...], preferred_element_type=F32)
    m = _sigmoid(ga_ref[...].astype(F32)) * br_a + _sigmoid(gb_ref[...].astype(F32)) * br_b
    acc_ref[...] += jnp.dot(m.astype(BF16), wo_ref[...], preferred_element_type=F32)

    @pl.when(j == pl.num_programs(1) - 1)
    def _():
        o_ref[...] = h_ref[...] + _rms(acc_ref[...], g_ref[...])


def _merge(a_in, b_in, proj, h, g_post, w_a, w_b, w_o):
    m, d = h.shape
    tm = _pick(m, (512, 256, 128))
    tn = 512
    nn = d // tn
    return pl.pallas_call(
        _merge_kernel,
        grid=(m // tm, nn),
        in_specs=[
            pl.BlockSpec((tm, d), lambda i, j: (i, 0)),
            pl.BlockSpec((tm, d), lambda i, j: (i, 0)),
            pl.BlockSpec((tm, tn), lambda i, j: (i, 6 * nn + j)),
            pl.BlockSpec((tm, tn), lambda i, j: (i, 7 * nn + j)),
            pl.BlockSpec((tm, d), lambda i, j: (i, 0)),
            pl.BlockSpec((1, d), lambda i, j: (0, 0)),
            pl.BlockSpec((d, tn), lambda i, j: (0, j)),
            pl.BlockSpec((d, tn), lambda i, j: (0, j)),
            pl.BlockSpec((tn, d), lambda i, j: (j, 0)),
        ],
        out_specs=pl.BlockSpec((tm, d), lambda i, j: (i, 0)),
        out_shape=jax.ShapeDtypeStruct((m, d), F32),
        scratch_shapes=[pltpu.VMEM((tm, d), F32)],
        compiler_params=_cparams(("parallel", "arbitrary")),
        name="merge",
    )(a_in, b_in, proj, proj, h, g_post, w_a, w_b, w_o)


def _ple_kernel(h_ref, p_ref, gpre_ref, gpost_ref, wg_ref, wp_ref, o_ref):
    h = h_ref[...]
    gate = jnp.dot(_rms(h, gpre_ref[...]).astype(BF16), wg_ref[...], preferred_element_type=F32)
    emb = jnp.dot(p_ref[...].astype(BF16), wp_ref[...], preferred_element_type=F32)
    o_ref[...] = h + _rms(emb * _sigmoid(gate), gpost_ref[...])


def _ple(h, p, g_pre, g_post, w_gate, w_proj):
    m, d = h.shape
    pd = p.shape[1]
    tm = _pick(m, (512, 256, 128))
    return pl.pallas_call(
        _ple_kernel,
        grid=(m // tm,),
        in_specs=[
            pl.BlockSpec((tm, d), lambda i: (i, 0)),
            pl.BlockSpec((tm, pd), lambda i: (i, 0)),
            pl.BlockSpec((1, d), lambda i: (0, 0)),
            pl.BlockSpec((1, d), lambda i: (0, 0)),
            pl.BlockSpec((d, d), lambda i: (0, 0)),
            pl.BlockSpec((pd, d), lambda i: (0, 0)),
        ],
        out_specs=pl.BlockSpec((tm, d), lambda i: (i, 0)),
        out_shape=jax.ShapeDtypeStruct((m, d), F32),
        compiler_params=_cparams(("parallel",)),
        name="ple",
    )(h, p, g_pre, g_post, w_gate, w_proj)


def _layer(x, p, conv_buf, h0, s0, reset_first, w):
    nb, t, d = x.shape
    m = nb * t
    xf = x.reshape(m, d)
    h1 = _ffn(xf, w["g_ffn1_pre"], w["g_ffn1_post"], w["w_ffn1_in"], w["w_ffn1_out"])
    proj = _proj(h1, w["g_mix_pre"], w["w_in"])
    cbuf = jnp.concatenate(
        [jnp.zeros((nb, SUBLANES - (CONV_W - 1), d), F32), conv_buf.astype(F32)], axis=1)
    a_in, h_last = _rglru(proj, nb, t, cbuf, h0.reshape(nb, 1, d), w["rg_conv_w"], w["rg_conv_b"],
                          w["rg_wax"], w["rg_bax"], w["rg_a_param"], reset_first)
    b_in, s_new = _hgrn2(proj, nb, t, s0, w["hg_lower_bound"], w["hg_norm_g"])
    h2 = _merge(a_in, b_in, proj, h1, w["g_mix_post"], w["w_branch_a"], w["w_branch_b"], w["w_out"])
    h3 = _ffn(h2, w["g_ffn2_pre"], w["g_ffn2_post"], w["w_ffn2_in"], w["w_ffn2_out"])
    y = _ple(h3, p.reshape(m, p.shape[-1]), w["g_ple_pre"], w["g_ple_post"], w["w_ple_gate"], w["w_ple_proj"])
    conv_new = proj.reshape(nb, t, -1)[:, t - (CONV_W - 1):, :d].astype(F32)
    return y.reshape(nb, t, d), conv_new, h_last.reshape(nb, d), s_new


def kernel(x_prompt, x_sample, p_prompt, p_sample, state_rglru_conv, state_rglru_h, state_hgrn2, g_ffn1_pre, g_ffn1_post, w_ffn1_in, w_ffn1_out, g_mix_pre, g_mix_post, w_in, rg_conv_w, rg_conv_b, rg_w_a, rg_b_a, rg_w_x, rg_b_x, rg_a_param, hg_lower_bound, hg_norm_g, w_branch_a, w_branch_b, w_out, g_ffn2_pre, g_ffn2_post, w_ffn2_in, w_ffn2_out, g_ple_pre, g_ple_post, w_ple_gate, w_ple_proj):
    assert g_ffn1_pre.shape[0] == 1 and hg_lower_bound.shape[0] == 2, "written for a single trunk layer"
    assert x_prompt.shape[1] >= CONV_W - 1 and x_sample.shape[1] >= CONV_W - 1
    l = 0
    bf = lambda a: a.astype(BF16)
    w = {
        "g_ffn1_pre": g_ffn1_pre[l][None], "g_ffn1_post": g_ffn1_post[l][None],
        "w_ffn1_in": bf(w_ffn1_in[l]), "w_ffn1_out": bf(w_ffn1_out[l]),
        "g_mix_pre": g_mix_pre[l][None], "g_mix_post": g_mix_post[l][None],
        "w_in": bf(w_in[l]),
        "rg_conv_w": rg_conv_w[l], "rg_conv_b": rg_conv_b[l][None],
        "rg_wax": bf(jnp.concatenate([rg_w_a[l], rg_w_x[l]], axis=-1)),
        "rg_bax": jnp.concatenate([rg_b_a[l], rg_b_x[l]], axis=-1)[:, None, :],
        "rg_a_param": rg_a_param[l][None],
        "hg_lower_bound": hg_lower_bound, "hg_norm_g": hg_norm_g[l][None],
        "w_branch_a": bf(w_branch_a[l]), "w_branch_b": bf(w_branch_b[l]), "w_out": bf(w_out[l]),
        "g_ffn2_pre": g_ffn2_pre[l][None], "g_ffn2_post": g_ffn2_post[l][None],
        "w_ffn2_in": bf(w_ffn2_in[l]), "w_ffn2_out": bf(w_ffn2_out[l]),
        "g_ple_pre": g_ple_pre[l][None], "g_ple_post": g_ple_post[l][None],
        "w_ple_gate": bf(w_ple_gate[l]), "w_ple_proj": bf(w_ple_proj[l]),
    }
    bp = x_prompt.shape[0]
    d = x_prompt.shape[-1]
    nh = state_hgrn2.shape[2]
    yp, conv_p, h_p, s_p = _layer(
        x_prompt, p_prompt[l],
        jnp.zeros((bp, CONV_W - 1, d), F32), jnp.zeros((bp, d), F32),
        jnp.zeros((bp, nh, HEAD, HEAD), F32), True, w)
    ys, conv_s, h_s, s_s = _layer(
        x_sample, p_sample[l], state_rglru_conv[l], state_rglru_h[l], state_hgrn2[l], False, w)
    return (yp, ys, conv_p[None], h_p[None], s_p[None], conv_s[None], h_s[None], s_s[None])
```

```python
import functools

import jax
import jax.numpy as jnp
import numpy as np
from jax import lax
from jax.experimental import pallas as pl
from jax.experimental.pallas import tpu as pltpu

F32 = jnp.float32
BF16 = jnp.bfloat16

EPS = 1e-6
RG_C = 8.0
CONV_W = 4
HEAD = 128
CHUNK = 64
SUBLANES = 8
VMEM_LIMIT = 56 * 1024 * 1024
LOG2E = 1.4426950408889634


def _cparams(sem, flags=None):
    return pltpu.CompilerParams(dimension_semantics=sem, vmem_limit_bytes=VMEM_LIMIT, flags=flags)


def _rms(x, g):
    return x * lax.rsqrt(jnp.mean(x * x, axis=-1, keepdims=True) + EPS) * g


def _sigmoid(x):
    return 0.5 + 0.5 * jnp.tanh(0.5 * x)


def _silu(x):
    h = 0.5 * x
    return h + h * jnp.tanh(h)


def _gelu_tanh(x):
    h = 0.5 * x
    return h + h * jnp.tanh(x * (0.7978845608028654 + (0.7978845608028654 * 0.044715) * (x * x)))


def _log1p(z):
    w = 1.0 + z
    return jnp.where(w == 1.0, z, jnp.log(w) * (z / jnp.where(w == 1.0, 1.0, w - 1.0)))


def _pick(n, prefs):
    for p in prefs:
        if n % p == 0:
            return p
    raise ValueError(f"no tile for {n} in {prefs}")


def _ffn_kernel(x_ref, gpre_ref, gpost_ref, wg_ref, wu_ref, wo_ref, o_ref, xn_ref, acc_ref):
    j = pl.program_id(1)

    @pl.when(j == 0)
    def _():
        xn_ref[...] = _rms(x_ref[...], gpre_ref[...]).astype(BF16)
        acc_ref[...] = jnp.zeros_like(acc_ref)

    xn = xn_ref[...]
    gate = jnp.dot(xn, wg_ref[...], preferred_element_type=F32)
    up = jnp.dot(xn, wu_ref[...], preferred_element_type=F32)
    act = (_silu(gate) * up).astype(BF16)
    acc_ref[...] += jnp.dot(act, wo_ref[...], preferred_element_type=F32)

    @pl.when(j == pl.num_programs(1) - 1)
    def _():
        o_ref[...] = x_ref[...] + 0.5 * _rms(acc_ref[...], gpost_ref[...])


def _ffn(x, g_pre, g_post, w_in, w_out):
    m, d = x.shape
    f = w_out.shape[0]
    tm = _pick(m, (512, 256, 128))
    tf = _pick(f, (512, 256, 128))
    nf = f // tf
    return pl.pallas_call(
        _ffn_kernel,
        grid=(m // tm, nf),
        in_specs=[
            pl.BlockSpec((tm, d), lambda i, j: (i, 0)),
            pl.BlockSpec((1, d), lambda i, j: (0, 0)),
            pl.BlockSpec((1, d), lambda i, j: (0, 0)),
            pl.BlockSpec((d, tf), lambda i, j: (0, j)),
            pl.BlockSpec((d, tf), lambda i, j: (0, j + nf)),
            pl.BlockSpec((tf, d), lambda i, j: (j, 0)),
        ],
        out_specs=pl.BlockSpec((tm, d), lambda i, j: (i, 0)),
        out_shape=jax.ShapeDtypeStruct((m, d), F32),
        scratch_shapes=[pltpu.VMEM((tm, d), BF16), pltpu.VMEM((tm, d), F32)],
        compiler_params=_cparams(("parallel", "arbitrary")),
        name="ffn",
    )(x, g_pre, g_post, w_in, w_in, w_out)


def _proj_kernel(x_ref, g_ref, w_ref, o_ref, xn_ref):
    @pl.when(pl.program_id(1) == 0)
    def _():
        xn_ref[...] = _rms(x_ref[...], g_ref[...]).astype(BF16)

    o_ref[...] = jnp.dot(xn_ref[...], w_ref[...], preferred_element_type=F32).astype(o_ref.dtype)


def _proj(x, g, w):
    m, d = x.shape
    n = w.shape[1]
    tm = _pick(m, (1024, 512, 256, 128))
    tn = _pick(n, (2048, 1024, 512, 256, 128))
    return pl.pallas_call(
        _proj_kernel,
        grid=(m // tm, n // tn),
        in_specs=[
            pl.BlockSpec((tm, d), lambda i, j: (i, 0)),
            pl.BlockSpec((1, d), lambda i, j: (0, 0)),
            pl.BlockSpec((d, tn), lambda i, j: (0, j)),
        ],
        out_specs=pl.BlockSpec((tm, tn), lambda i, j: (i, j)),
        out_shape=jax.ShapeDtypeStruct((m, n), BF16),
        scratch_shapes=[pltpu.VMEM((tm, d), BF16)],
        compiler_params=_cparams(("parallel", "arbitrary")),
        name="proj",
    )(x, g, w)


RG_GROUP = 16
RG_PAD = SUBLANES


def _rglru_kernel(x_ref, y_ref, cbuf_ref, h0_ref, cw_ref, cb_ref, wax_ref, bax_ref, ap_ref,
                  o_ref, hlast_ref, xp_ref, tail_ref, a_ref, u_ref, ix0_ref, carry_ref,
                  *, tb, cols, reset_first):
    t = pl.program_id(2)
    pad = SUBLANES
    nslab = cols // HEAD
    pitch = tb + RG_PAD
    slabs = [slice(h * HEAD, (h + 1) * HEAD) for h in range(nslab)]

    @pl.when(t == 0)
    def _():
        tail_ref[...] = cbuf_ref[...]
        carry_ref[...] = h0_ref[...]

    cw = cw_ref[...]
    cb = cb_ref[...]
    slope = (-RG_C) * (jnp.maximum(ap_ref[...], 0.0) + _log1p(jnp.exp(-jnp.abs(ap_ref[...]))))

    def gates_of(seq, carry):
        xp_ref[0:pad, :] = tail_ref[seq]
        xp_ref[pad:pad + tb, :] = x_ref[seq].astype(F32)
        xc = cb + cw[0:1, :] * xp_ref[pad:pad + tb, :]
        for j in range(1, CONV_W):
            xc = xc + cw[j:j + 1, :] * xp_ref[pad - j:pad - j + tb, :]
        tail_ref[seq] = xp_ref[tb:tb + pad, :]
        r0 = pl.multiple_of(seq * pitch, SUBLANES)
        for h, sl in enumerate(slabs):
            xh = xc[:, sl]
            gates = jnp.dot(xh.astype(BF16), wax_ref[h], preferred_element_type=F32) + bax_ref[h]
            r = _sigmoid(gates[:, :HEAD])
            ix = _sigmoid(gates[:, HEAD:]) * xh
            av = jnp.exp(r * slope[:, sl])
            gap = 1.0 - av * av
            mult = jnp.where(gap > 0.0, gap * lax.rsqrt(gap), 0.0)
            a_ref[h, pl.ds(r0, tb), :] = av
            u_ref[h, pl.ds(r0, tb), :] = mult * ix
            if reset_first:
                ix0_ref[h, pl.ds(seq, 1), :] = ix[0:1, :]
        return carry

    lax.fori_loop(0, RG_GROUP, gates_of, 0)

    if reset_first:
        @pl.when(t == 0)
        def _():
            for h in range(nslab):
                a_ref[h, pl.ds(0, RG_GROUP, stride=pitch), :] = jnp.zeros((RG_GROUP, HEAD), F32)
                u_ref[h, pl.ds(0, RG_GROUP, stride=pitch), :] = ix0_ref[h]

    def step(tt, hs):
        out = []
        for h in range(nslab):
            rows = pl.ds(tt, RG_GROUP, stride=pitch)
            hcur = a_ref[h, rows, :] * hs[h] + u_ref[h, rows, :]
            u_ref[h, rows, :] = hcur
            out.append(hcur)
        return tuple(out)

    hs = lax.fori_loop(0, tb, step, tuple(carry_ref[:, sl] for sl in slabs), unroll=8)
    last = jnp.concatenate(hs, axis=1)
    carry_ref[...] = last
    hlast_ref[...] = last

    def emit(seq, carry):
        r0 = pl.multiple_of(seq * pitch, SUBLANES)
        hseq = jnp.concatenate([u_ref[h, pl.ds(r0, tb), :] for h in range(nslab)], axis=1)
        o_ref[seq] = (_gelu_tanh(y_ref[seq].astype(F32)) * hseq).astype(o_ref.dtype)
        return carry

    lax.fori_loop(0, RG_GROUP, emit, 0)


def _rglru(proj, nb, t, conv_buf, h0, cw, cb, wax, bax, a_param, reset_first):
    d = cw.shape[1]
    cols = 512
    nslab = cols // HEAD
    tb = _pick(t, (256, 128, 64))
    nt = t // tb
    ncol = d // cols
    assert nb % RG_GROUP == 0
    kern = functools.partial(_rglru_kernel, tb=tb, cols=cols, reset_first=reset_first)
    proj3 = proj.reshape(nb, t, proj.shape[1])
    out, hlast = pl.pallas_call(
        kern,
        grid=(nb // RG_GROUP, ncol, nt),
        in_specs=[
            pl.BlockSpec((RG_GROUP, tb, cols), lambda g, c, s: (g, s, c)),
            pl.BlockSpec((RG_GROUP, tb, cols), lambda g, c, s: (g, s, ncol + c)),
            pl.BlockSpec((RG_GROUP, SUBLANES, cols), lambda g, c, s: (g, 0, c)),
            pl.BlockSpec((RG_GROUP, cols), lambda g, c, s: (g, c)),
            pl.BlockSpec((CONV_W, cols), lambda g, c, s: (0, c)),
            pl.BlockSpec((1, cols), lambda g, c, s: (0, c)),
            pl.BlockSpec((nslab, HEAD, 2 * HEAD), lambda g, c, s: (c, 0, 0)),
            pl.BlockSpec((nslab, 1, 2 * HEAD), lambda g, c, s: (c, 0, 0)),
            pl.BlockSpec((1, cols), lambda g, c, s: (0, c)),
        ],
        out_specs=[
            pl.BlockSpec((RG_GROUP, tb, cols), lambda g, c, s: (g, s, c)),
            pl.BlockSpec((RG_GROUP, cols), lambda g, c, s: (g, c)),
        ],
        out_shape=[
            jax.ShapeDtypeStruct((nb, t, d), BF16),
            jax.ShapeDtypeStruct((nb, d), F32),
        ],
        scratch_shapes=[
            pltpu.VMEM((tb + SUBLANES, cols), F32),
            pltpu.VMEM((RG_GROUP, SUBLANES, cols), F32),
            pltpu.VMEM((nslab, RG_GROUP * (tb + RG_PAD), HEAD), F32),
            pltpu.VMEM((nslab, RG_GROUP * (tb + RG_PAD), HEAD), F32),
            pltpu.VMEM((nslab, RG_GROUP, HEAD), F32),
            pltpu.VMEM((RG_GROUP, cols), F32),
        ],
        compiler_params=_cparams(("parallel", "parallel", "arbitrary")),
        name="rglru",
    )(proj3, proj3, conv_buf, h0, cw, cb, wax, bax, a_param)
    return out.reshape(nb * t, d), hlast


LEVELS = (32, 16, 8, 4, 2, 1)


def _hgrn2_masks():
    t = np.arange(CHUNK)[:, None]
    s = np.arange(CHUNK)[None, :]
    out = []
    for half in LEVELS:
        same = (t // (2 * half)) == (s // (2 * half))
        out.append(same & (t % (2 * half) >= half) & (s % (2 * half) < half))
    out.append(t == s)
    return jnp.asarray(np.stack(out), F32)


def _row_bcast(b, first, step):
    span = max(step, SUBLANES)
    return jnp.concatenate(
        [jnp.broadcast_to(b[r:r + 1, :], (span, b.shape[1])) for r in range(first, b.shape[0], span)], axis=0)


def _level_ref_rows(b, half, row):
    if 2 * half >= SUBLANES:
        return _row_bcast(b, half - 1, 2 * half)
    if half == 2:
        return jnp.where(jnp.bitwise_and(row, 4) == 0, _row_bcast(b, 1, SUBLANES), _row_bcast(b, 5, SUBLANES))
    assert half == 1
    return jnp.where(jnp.bitwise_and(row, 1) == 0, b, pltpu.roll(b, 1, 0))


def _hgrn2_kernel(q_ref, f_ref, v_ref, g_ref, s0_ref, lbp_ref, gn_ref, lvl_ref,
                  o_ref, snew_ref, st_ref, *, tb, hpb, per_trip):
    t = pl.program_id(2)
    width = hpb * HEAD
    rows = per_trip * CHUNK

    @pl.when(t == 0)
    def _():
        for h in range(hpb):
            st_ref[h] = s0_ref[0, h].T

    lbp = lbp_ref[...]
    le = jnp.exp(lbp - jnp.max(lbp, axis=0, keepdims=True))
    lb = le[0:1, :] / jnp.sum(le, axis=0, keepdims=True)
    f_mid = 0.5 * (1.0 + lb)
    f_amp = 0.5 * (1.0 - lb)
    gn = gn_ref[...]
    row = lax.broadcasted_iota(jnp.int32, (rows, width), 0)
    his = [jnp.bitwise_and(row, 2 * half - 1) >= half for half in LEVELS]
    row16 = row.astype(jnp.int16)
    his16 = [jnp.bitwise_and(row16, 2 * half - 1) >= half for half in LEVELS]
    tr = lax.broadcasted_iota(jnp.int32, (rows, rows), 0)
    tc = lax.broadcasted_iota(jnp.int32, (rows, rows), 1)
    tri = jnp.logical_and(tr >= tc, tr // CHUNK == tc // CHUNK).astype(BF16)
    nt_dims = (((1,), (1,)), ((), ()))
    heads = [slice(h * HEAD, (h + 1) * HEAD) for h in range(hpb)]
    chunks = [slice(u * CHUNK, (u + 1) * CHUNK) for u in range(per_trip)]

    def body(c, carry):
        r0 = pl.multiple_of(c * rows, rows)
        qp = q_ref[pl.ds(r0, rows), :].astype(F32)
        fp = f_ref[pl.ds(r0, rows), :].astype(F32)
        v = v_ref[pl.ds(r0, rows), :]
        gp = g_ref[pl.ds(r0, rows), :].astype(F32)
        sts = [st_ref[h] for h in range(hpb)]
        f = f_mid + f_amp * jnp.tanh(0.5 * fp)
        k = 1.0 - f
        q = _silu(qp)
        lf = jnp.log(f)
        lf_hi = lf.astype(BF16)
        lf_lo = (lf - lf_hi.astype(F32)).astype(BF16)
        b2 = jnp.dot(tri, jnp.concatenate([lf_hi, lf_lo], axis=1), preferred_element_type=F32)
        b = b2[:, :width] + b2[:, width:]
        qe = (q * jnp.exp(b)).astype(BF16)
        q16 = q.astype(BF16)
        k16 = k.astype(BF16)
        eye = lvl_ref[len(LEVELS)]
        scores = [[eye * lax.dot_general(q16[cs, sl], k16[cs, sl], nt_dims, preferred_element_type=F32)
                   for sl in heads] for cs in chunks]
        for i, half in enumerate(LEVELS):
            e = jnp.exp2((b - _level_ref_rows(b, half, row)) * jnp.where(his[i], LOG2E, -LOG2E))
            z = jnp.where(his16[i], q16, k16) * e.astype(BF16)
            mask = lvl_ref[i]
            for u, cs in enumerate(chunks):
                for h, sl in enumerate(heads):
                    gram = lax.dot_general(z[cs, sl], z[cs, sl], nt_dims, preferred_element_type=F32)
                    scores[u][h] = scores[u][h] + gram * mask
        blast = _row_bcast(b, CHUNK - 1, CHUNK)
        kd = (k * jnp.exp(blast - b)).astype(BF16)
        eblast = jnp.exp(blast)
        vt = [[v[cs, sl].astype(F32).T.astype(BF16) for sl in heads] for cs in chunks]
        out_rows = []
        for u, cs in enumerate(chunks):
            outs = []
            for h, sl in enumerate(heads):
                o = lax.dot_general(qe[cs, sl], sts[h].astype(BF16), nt_dims, preferred_element_type=F32)
                o = o + jnp.dot(scores[u][h].astype(BF16), v[cs, sl], preferred_element_type=F32)
                outs.append(_rms(o, gn[:, sl]))
                sts[h] = (sts[h] * eblast[u * CHUNK:u * CHUNK + 1, sl]
                          + jnp.dot(vt[u][h], kd[cs, sl], preferred_element_type=F32))
            out_rows.append(jnp.concatenate(outs, axis=1))
        o_all = jnp.concatenate(out_rows, axis=0) if per_trip > 1 else out_rows[0]
        o_ref[pl.ds(r0, rows), :] = (o_all * _silu(gp)).astype(o_ref.dtype)
        for h in range(hpb):
            st_ref[h] = sts[h]
        return carry

    lax.fori_loop(0, tb // rows, body, 0)
    for h in range(hpb):
        snew_ref[0, h] = st_ref[h].T


def _hgrn2(proj, nb, t, s0, lb_param, gn):
    nh = s0.shape[1]
    d = nh * HEAD
    hpb = 8
    cols = hpb * HEAD
    ng = nh // hpb
    tb = _pick(t, (1024, 512, 256, 128, 64))
    nt = t // tb
    per_trip = _pick(tb // CHUNK, (4, 2, 1))
    kern = functools.partial(_hgrn2_kernel, tb=tb, hpb=hpb, per_trip=per_trip)
    lvl = _hgrn2_masks()

    def col_spec(group):
        return pl.BlockSpec((tb, cols), lambda b, h, s: (b * nt + s, group * ng + h))

    return pl.pallas_call(
        kern,
        grid=(nb, ng, nt),
        in_specs=[
            col_spec(2), col_spec(3), col_spec(4), col_spec(5),
            pl.BlockSpec((1, hpb, HEAD, HEAD), lambda b, h, s: (b, h, 0, 0)),
            pl.BlockSpec((lb_param.shape[0], cols), lambda b, h, s: (0, h)),
            pl.BlockSpec((1, cols), lambda b, h, s: (0, h)),
            pl.BlockSpec(lvl.shape, lambda b, h, s: (0, 0, 0)),
        ],
        out_specs=[
            pl.BlockSpec((tb, cols), lambda b, h, s: (b * nt + s, h)),
            pl.BlockSpec((1, hpb, HEAD, HEAD), lambda b, h, s: (b, h, 0, 0)),
        ],
        out_shape=[
            jax.ShapeDtypeStruct((nb * t, d), BF16),
            jax.ShapeDtypeStruct(s0.shape, F32),
        ],
        scratch_shapes=[pltpu.VMEM((hpb, HEAD, HEAD), F32)],
        compiler_params=_cparams(("parallel", "parallel", "arbitrary")),
        name="hgrn2",
    )(proj, proj, proj, proj, s0, lb_param, gn, lvl)


def _merge_kernel(a_ref, b_ref, ga_ref, gb_ref, h_ref, g_ref, wa_ref, wb_ref, wo_ref, o_ref, acc_ref):
    j = pl.program_id(1)

    @pl.when(j == 0)
    def _():
        acc_ref[...] = jnp.zeros_like(acc_ref)

    br_a = jnp.dot(a_ref[...], wa_ref[...], preferred_element_type=F32)
    br_b = jnp.dot(b_ref[...], wb_ref[...], preferred_element_type=F32)
    m = _sigmoid(ga_ref[...].astype(F32)) * br_a + _sigmoid(gb_ref[...].astype(F32)) * br_b
    acc_ref[...] += jnp.dot(m.astype(BF16), wo_ref[...], preferred_element_type=F32)

    @pl.when(j == pl.num_programs(1) - 1)
    def _():
        o_ref[...] = h_ref[...] + _rms(acc_ref[...], g_ref[...])


def _merge(a_in, b_in, proj, h, g_post, w_a, w_b, w_o):
    m, d = h.shape
    tm = _pick(m, (512, 256, 128))
    tn = 512
    nn = d // tn
    return pl.pallas_call(
        _merge_kernel,
        grid=(m // tm, nn),
        in_specs=[
            pl.BlockSpec((tm, d), lambda i, j: (i, 0)),
            pl.BlockSpec((tm, d), lambda i, j: (i, 0)),
            pl.BlockSpec((tm, tn), lambda i, j: (i, 6 * nn + j)),
            pl.BlockSpec((tm, tn), lambda i, j: (i, 7 * nn + j)),
            pl.BlockSpec((tm, d), lambda i, j: (i, 0)),
            pl.BlockSpec((1, d), lambda i, j: (0, 0)),
            pl.BlockSpec((d, tn), lambda i, j: (0, j)),
            pl.BlockSpec((d, tn), lambda i, j: (0, j)),
            pl.BlockSpec((tn, d), lambda i, j: (j, 0)),
        ],
        out_specs=pl.BlockSpec((tm, d), lambda i, j: (i, 0)),
        out_shape=jax.ShapeDtypeStruct((m, d), F32),
        scratch_shapes=[pltpu.VMEM((tm, d), F32)],
        compiler_params=_cparams(("parallel", "arbitrary")),
        name="merge",
    )(a_in, b_in, proj, proj, h, g_post, w_a, w_b, w_o)


def _ple_kernel(h_ref, p_ref, gpre_ref, gpost_ref, wg_ref, wp_ref, o_ref):
    h = h_ref[...]
    gate = jnp.dot(_rms(h, gpre_ref[...]).astype(BF16), wg_ref[...], preferred_element_type=F32)
    emb = jnp.dot(p_ref[...].astype(BF16), wp_ref[...], preferred_element_type=F32)
    o_ref[...] = h + _rms(emb * _sigmoid(gate), gpost_ref[...])


def _ple(h, p, g_pre, g_post, w_gate, w_proj):
    m, d = h.shape
    pd = p.shape[1]
    tm = _pick(m, (512, 256, 128))
    return pl.pallas_call(
        _ple_kernel,
        grid=(m // tm,),
        in_specs=[
            pl.BlockSpec((tm, d), lambda i: (i, 0)),
            pl.BlockSpec((tm, pd), lambda i: (i, 0)),
            pl.BlockSpec((1, d), lambda i: (0, 0)),
            pl.BlockSpec((1, d), lambda i: (0, 0)),
            pl.BlockSpec((d, d), lambda i: (0, 0)),
            pl.BlockSpec((pd, d), lambda i: (0, 0)),
        ],
        out_specs=pl.BlockSpec((tm, d), lambda i: (i, 0)),
        out_shape=jax.ShapeDtypeStruct((m, d), F32),
        compiler_params=_cparams(("parallel",)),
        name="ple",
    )(h, p, g_pre, g_post, w_gate, w_proj)


def _layer(x, p, conv_buf, h0, s0, reset_first, w):
    nb, t, d = x.shape
    m = nb * t
    xf = x.reshape(m, d)
    h1 = _ffn(xf, w["g_ffn1_pre"], w["g_ffn1_post"], w["w_ffn1_in"], w["w_ffn1_out"])
    proj = _proj(h1, w["g_mix_pre"], w["w_in"])
    cbuf = jnp.concatenate(
        [jnp.zeros((nb, SUBLANES - (CONV_W - 1), d), F32), conv_buf.astype(F32)], axis=1)
    a_in, h_last = _rglru(proj, nb, t, cbuf, h0, w["rg_conv_w"], w["rg_conv_b"],
                          w["rg_wax"], w["rg_bax"], w["rg_a_param"], reset_first)
    b_in, s_new = _hgrn2(proj, nb, t, s0, w["hg_lower_bound"], w["hg_norm_g"])
    h2 = _merge(a_in, b_in, proj, h1, w["g_mix_post"], w["w_branch_a"], w["w_branch_b"], w["w_out"])
    h3 = _ffn(h2, w["g_ffn2_pre"], w["g_ffn2_post"], w["w_ffn2_in"], w["w_ffn2_out"])
    y = _ple(h3, p.reshape(m, p.shape[-1]), w["g_ple_pre"], w["g_ple_post"], w["w_ple_gate"], w["w_ple_proj"])
    conv_new = proj.reshape(nb, t, -1)[:, t - (CONV_W - 1):, :d].astype(F32)
    return y.reshape(nb, t, d), conv_new, h_last, s_new


def kernel(x_prompt, x_sample, p_prompt, p_sample, state_rglru_conv, state_rglru_h, state_hgrn2, g_ffn1_pre, g_ffn1_post, w_ffn1_in, w_ffn1_out, g_mix_pre, g_mix_post, w_in, rg_conv_w, rg_conv_b, rg_w_a, rg_b_a, rg_w_x, rg_b_x, rg_a_param, hg_lower_bound, hg_norm_g, w_branch_a, w_branch_b, w_out, g_ffn2_pre, g_ffn2_post, w_ffn2_in, w_ffn2_out, g_ple_pre, g_ple_post, w_ple_gate, w_ple_proj):
    assert g_ffn1_pre.shape[0] == 1 and hg_lower_bound.shape[0] == 2, "written for a single trunk layer"
    assert x_prompt.shape[1] >= CONV_W - 1 and x_sample.shape[1] >= CONV_W - 1
    l = 0
    bf = lambda a: a.astype(BF16)
    w = {
        "g_ffn1_pre": g_ffn1_pre[l][None], "g_ffn1_post": g_ffn1_post[l][None],
        "w_ffn1_in": bf(w_ffn1_in[l]), "w_ffn1_out": bf(w_ffn1_out[l]),
        "g_mix_pre": g_mix_pre[l][None], "g_mix_post": g_mix_post[l][None],
        "w_in": bf(w_in[l]),
        "rg_conv_w": rg_conv_w[l], "rg_conv_b": rg_conv_b[l][None],
        "rg_wax": bf(jnp.concatenate([rg_w_a[l], rg_w_x[l]], axis=-1)),
        "rg_bax": jnp.concatenate([rg_b_a[l], rg_b_x[l]], axis=-1)[:, None, :],
        "rg_a_param": rg_a_param[l][None],
        "hg_lower_bound": hg_lower_bound, "hg_norm_g": hg_norm_g[l][None],
        "w_branch_a": bf(w_branch_a[l]), "w_branch_b": bf(w_branch_b[l]), "w_out": bf(w_out[l]),
        "g_ffn2_pre": g_ffn2_pre[l][None], "g_ffn2_post": g_ffn2_post[l][None],
        "w_ffn2_in": bf(w_ffn2_in[l]), "w_ffn2_out": bf(w_ffn2_out[l]),
        "g_ple_pre": g_ple_pre[l][None], "g_ple_post": g_ple_post[l][None],
        "w_ple_gate": bf(w_ple_gate[l]), "w_ple_proj": bf(w_ple_proj[l]),
    }
    bp = x_prompt.shape[0]
    d = x_prompt.shape[-1]
    nh = state_hgrn2.shape[2]
    yp, conv_p, h_p, s_p = _layer(
        x_prompt, p_prompt[l],
        jnp.zeros((bp, CONV_W - 1, d), F32), jnp.zeros((bp, d), F32),
        jnp.zeros((bp, nh, HEAD, HEAD), F32), True, w)
    ys, conv_s, h_s, s_s = _layer(
        x_sample, p_sample[l], state_rglru_conv[l], state_rglru_h[l], state_hgrn2[l], False, w)
    return (yp, ys, conv_p[None], h_p[None], s_p[None], conv_s[None], h_s[None], s_s[None])
```

```python
import functools

import jax
import jax.numpy as jnp
import numpy as np
from jax import lax
from jax.experimental import pallas as pl
from jax.experimental.pallas import tpu as pltpu

F32 = jnp.float32
BF16 = jnp.bfloat16

EPS = 1e-6
RG_C = 8.0
CONV_W = 4
HEAD = 128
CHUNK = 64
SUBLANES = 8
VMEM_LIMIT = 56 * 1024 * 1024
LOG2E = 1.4426950408889634


def _cparams(sem, flags=None):
    return pltpu.CompilerParams(dimension_semantics=sem, vmem_limit_bytes=VMEM_LIMIT, flags=flags)


def _rms(x, g):
    return x * lax.rsqrt(jnp.mean(x * x, axis=-1, keepdims=True) + EPS) * g


def _sigmoid(x):
    return 0.5 + 0.5 * jnp.tanh(0.5 * x)


def _silu(x):
    h = 0.5 * x
    return h + h * jnp.tanh(h)


def _gelu_tanh(x):
    h = 0.5 * x
    return h + h * jnp.tanh(x * (0.7978845608028654 + (0.7978845608028654 * 0.044715) * (x * x)))


def _log1p(z):
    w = 1.0 + z
    return jnp.where(w == 1.0, z, jnp.log(w) * (z / jnp.where(w == 1.0, 1.0, w - 1.0)))


def _pick(n, prefs):
    for p in prefs:
        if n % p == 0:
            return p
    raise ValueError(f"no tile for {n} in {prefs}")


def _ffn_up_kernel(x_ref, gpre_ref, wg_ref, wu_ref, o_ref, xn_ref):
    @pl.when(pl.program_id(1) == 0)
    def _():
        xn_ref[...] = _rms(x_ref[...], gpre_ref[...]).astype(BF16)

    xn = xn_ref[...]
    gate = jnp.dot(xn, wg_ref[...], preferred_element_type=F32)
    up = jnp.dot(xn, wu_ref[...], preferred_element_type=F32)
    o_ref[...] = (_silu(gate) * up).astype(o_ref.dtype)


def _ffn_down_kernel(a_ref, x_ref, gpost_ref, wo_ref, o_ref):
    y = jnp.dot(a_ref[...], wo_ref[...], preferred_element_type=F32)
    o_ref[...] = x_ref[...] + 0.5 * _rms(y, gpost_ref[...])


def _ffn(x, g_pre, g_post, w_in, w_out):
    m, d = x.shape
    f = w_out.shape[0]
    tm = _pick(m, (1024, 512, 256, 128))
    tf = _pick(f, (512, 256, 128))
    nf = f // tf
    act = pl.pallas_call(
        _ffn_up_kernel,
        grid=(m // tm, nf),
        in_specs=[
            pl.BlockSpec((tm, d), lambda i, j: (i, 0)),
            pl.BlockSpec((1, d), lambda i, j: (0, 0)),
            pl.BlockSpec((d, tf), lambda i, j: (0, j)),
            pl.BlockSpec((d, tf), lambda i, j: (0, j + nf)),
        ],
        out_specs=pl.BlockSpec((tm, tf), lambda i, j: (i, j)),
        out_shape=jax.ShapeDtypeStruct((m, f), BF16),
        scratch_shapes=[pltpu.VMEM((tm, d), BF16)],
        compiler_params=_cparams(("parallel", "arbitrary")),
        name="ffn_up",
    )(x, g_pre, w_in, w_in)
    tm2 = _pick(m, (256, 128))
    return pl.pallas_call(
        _ffn_down_kernel,
        grid=(m // tm2,),
        in_specs=[
            pl.BlockSpec((tm2, f), lambda i: (i, 0)),
            pl.BlockSpec((tm2, d), lambda i: (i, 0)),
            pl.BlockSpec((1, d), lambda i: (0, 0)),
            pl.BlockSpec((f, d), lambda i: (0, 0), pipeline_mode=pl.Buffered(1)),
        ],
        out_specs=pl.BlockSpec((tm2, d), lambda i: (i, 0)),
        out_shape=jax.ShapeDtypeStruct((m, d), F32),
        compiler_params=_cparams(("parallel",)),
        name="ffn_down",
    )(act, x, g_post, w_out)


def _proj_kernel(x_ref, g_ref, w_ref, o_ref, xn_ref):
    @pl.when(pl.program_id(1) == 0)
    def _():
        xn_ref[...] = _rms(x_ref[...], g_ref[...]).astype(BF16)

    o_ref[...] = jnp.dot(xn_ref[...], w_ref[...], preferred_element_type=F32).astype(o_ref.dtype)


def _proj(x, g, w):
    m, d = x.shape
    n = w.shape[1]
    tm = _pick(m, (1024, 512, 256, 128))
    tn = _pick(n, (2048, 1024, 512, 256, 128))
    return pl.pallas_call(
        _proj_kernel,
        grid=(m // tm, n // tn),
        in_specs=[
            pl.BlockSpec((tm, d), lambda i, j: (i, 0)),
            pl.BlockSpec((1, d), lambda i, j: (0, 0)),
            pl.BlockSpec((d, tn), lambda i, j: (0, j)),
        ],
        out_specs=pl.BlockSpec((tm, tn), lambda i, j: (i, j)),
        out_shape=jax.ShapeDtypeStruct((m, n), BF16),
        scratch_shapes=[pltpu.VMEM((tm, d), BF16)],
        compiler_params=_cparams(("parallel", "arbitrary")),
        name="proj",
    )(x, g, w)


RG_GROUP = 16
RG_PAD = SUBLANES


def _rglru_kernel(x_ref, y_ref, cbuf_ref, h0_ref, cw_ref, cb_ref, wax_ref, bax_ref, ap_ref,
                  o_ref, hlast_ref, xp_ref, tail_ref, a_ref, u_ref, ix0_ref, carry_ref,
                  *, tb, cols, reset_first):
    t = pl.program_id(2)
    pad = SUBLANES
    nslab = cols // HEAD
    pitch = tb + RG_PAD
    slabs = [slice(h * HEAD, (h + 1) * HEAD) for h in range(nslab)]

    @pl.when(t == 0)
    def _():
        tail_ref[...] = cbuf_ref[...]
        carry_ref[...] = h0_ref[...]

    cw = cw_ref[...]
    cb = cb_ref[...]
    slope = (-RG_C) * (jnp.maximum(ap_ref[...], 0.0) + _log1p(jnp.exp(-jnp.abs(ap_ref[...]))))

    def gates_of(seq, carry):
        xp_ref[0:pad, :] = tail_ref[seq]
        xp_ref[pad:pad + tb, :] = x_ref[seq].astype(F32)
        xc = cb + cw[0:1, :] * xp_ref[pad:pad + tb, :]
        for j in range(1, CONV_W):
            xc = xc + cw[j:j + 1, :] * xp_ref[pad - j:pad - j + tb, :]
        tail_ref[seq] = xp_ref[tb:tb + pad, :]
        r0 = pl.multiple_of(seq * pitch, SUBLANES)
        for h, sl in enumerate(slabs):
            xh = xc[:, sl]
            gates = jnp.dot(xh.astype(BF16), wax_ref[h], preferred_element_type=F32) + bax_ref[h]
            r = _sigmoid(gates[:, :HEAD])
            ix = _sigmoid(gates[:, HEAD:]) * xh
            av = jnp.exp(r * slope[:, sl])
            gap = 1.0 - av * av
            mult = jnp.where(gap > 0.0, gap * lax.rsqrt(gap), 0.0)
            a_ref[h, pl.ds(r0, tb), :] = av
            u_ref[h, pl.ds(r0, tb), :] = mult * ix
            if reset_first:
                ix0_ref[h, pl.ds(seq, 1), :] = ix[0:1, :]
        return carry

    lax.fori_loop(0, RG_GROUP, gates_of, 0)

    if reset_first:
        @pl.when(t == 0)
        def _():
            for h in range(nslab):
                a_ref[h, pl.ds(0, RG_GROUP, stride=pitch), :] = jnp.zeros((RG_GROUP, HEAD), F32)
                u_ref[h, pl.ds(0, RG_GROUP, stride=pitch), :] = ix0_ref[h]

    def step(tt, hs):
        out = []
        for h in range(nslab):
            rows = pl.ds(tt, RG_GROUP, stride=pitch)
            hcur = a_ref[h, rows, :] * hs[h] + u_ref[h, rows, :]
            u_ref[h, rows, :] = hcur
            out.append(hcur)
        return tuple(out)

    hs = lax.fori_loop(0, tb, step, tuple(carry_ref[:, sl] for sl in slabs), unroll=8)
    last = jnp.concatenate(hs, axis=1)
    carry_ref[...] = last
    hlast_ref[...] = last

    def emit(seq, carry):
        r0 = pl.multiple_of(seq * pitch, SUBLANES)
        hseq = jnp.concatenate([u_ref[h, pl.ds(r0, tb), :] for h in range(nslab)], axis=1)
        o_ref[seq] = (_gelu_tanh(y_ref[seq].astype(F32)) * hseq).astype(o_ref.dtype)
        return carry

    lax.fori_loop(0, RG_GROUP, emit, 0)


def _rglru(proj, nb, t, conv_buf, h0, cw, cb, wax, bax, a_param, reset_first):
    d = cw.shape[1]
    cols = 512
    nslab = cols // HEAD
    tb = _pick(t, (256, 128, 64))
    nt = t // tb
    ncol = d // cols
    assert nb % RG_GROUP == 0
    kern = functools.partial(_rglru_kernel, tb=tb, cols=cols, reset_first=reset_first)
    proj3 = proj.reshape(nb, t, proj.shape[1])
    out, hlast = pl.pallas_call(
        kern,
        grid=(nb // RG_GROUP, ncol, nt),
        in_specs=[
            pl.BlockSpec((RG_GROUP, tb, cols), lambda g, c, s: (g, s, c)),
            pl.BlockSpec((RG_GROUP, tb, cols), lambda g, c, s: (g, s, ncol + c)),
            pl.BlockSpec((RG_GROUP, SUBLANES, cols), lambda g, c, s: (g, 0, c)),
            pl.BlockSpec((RG_GROUP, cols), lambda g, c, s: (g, c)),
            pl.BlockSpec((CONV_W, cols), lambda g, c, s: (0, c)),
            pl.BlockSpec((1, cols), lambda g, c, s: (0, c)),
            pl.BlockSpec((nslab, HEAD, 2 * HEAD), lambda g, c, s: (c, 0, 0)),
            pl.BlockSpec((nslab, 1, 2 * HEAD), lambda g, c, s: (c, 0, 0)),
            pl.BlockSpec((1, cols), lambda g, c, s: (0, c)),
        ],
        out_specs=[
            pl.BlockSpec((RG_GROUP, tb, cols), lambda g, c, s: (g, s, c)),
            pl.BlockSpec((RG_GROUP, cols), lambda g, c, s: (g, c)),
        ],
        out_shape=[
            jax.ShapeDtypeStruct((nb, t, d), BF16),
            jax.ShapeDtypeStruct((nb, d), F32),
        ],
        scratch_shapes=[
            pltpu.VMEM((tb + SUBLANES, cols), F32),
            pltpu.VMEM((RG_GROUP, SUBLANES, cols), F32),
            pltpu.VMEM((nslab, RG_GROUP * (tb + RG_PAD), HEAD), F32),
            pltpu.VMEM((nslab, RG_GROUP * (tb + RG_PAD), HEAD), F32),
            pltpu.VMEM((nslab, RG_GROUP, HEAD), F32),
            pltpu.VMEM((RG_GROUP, cols), F32),
        ],
        compiler_params=_cparams(("parallel", "parallel", "arbitrary")),
        name="rglru",
    )(proj3, proj3, conv_buf, h0, cw, cb, wax, bax, a_param)
    return out.reshape(nb * t, d), hlast


LEVELS = (32, 16, 8, 4, 2, 1)


def _hgrn2_masks():
    t = np.arange(CHUNK)[:, None]
    s = np.arange(CHUNK)[None, :]
    out = []
    for half in LEVELS:
        same = (t // (2 * half)) == (s // (2 * half))
        out.append(same & (t % (2 * half) >= half) & (s % (2 * half) < half))
    out.append(t == s)
    return jnp.asarray(np.stack(out), F32)


def _row_bcast(b, first, step):
    span = max(step, SUBLANES)
    return jnp.concatenate(
        [jnp.broadcast_to(b[r:r + 1, :], (span, b.shape[1])) for r in range(first, b.shape[0], span)], axis=0)


def _level_ref_rows(b, half, row):
    if 2 * half >= SUBLANES:
        return _row_bcast(b, half - 1, 2 * half)
    if half == 2:
        return jnp.where(jnp.bitwise_and(row, 4) == 0, _row_bcast(b, 1, SUBLANES), _row_bcast(b, 5, SUBLANES))
    assert half == 1
    return jnp.where(jnp.bitwise_and(row, 1) == 0, b, pltpu.roll(b, 1, 0))


def _hgrn2_kernel(q_ref, f_ref, v_ref, g_ref, s0_ref, lbp_ref, gn_ref, lvl_ref,
                  o_ref, snew_ref, st_ref, *, tb, hpb, per_trip):
    t = pl.program_id(2)
    width = hpb * HEAD
    rows = per_trip * CHUNK

    @pl.when(t == 0)
    def _():
        for h in range(hpb):
            st_ref[h] = s0_ref[0, h].T

    lbp = lbp_ref[...]
    le = jnp.exp(lbp - jnp.max(lbp, axis=0, keepdims=True))
    lb = le[0:1, :] / jnp.sum(le, axis=0, keepdims=True)
    f_mid = 0.5 * (1.0 + lb)
    f_amp = 0.5 * (1.0 - lb)
    gn = gn_ref[...]
    row = lax.broadcasted_iota(jnp.int32, (rows, width), 0)
    his = [jnp.bitwise_and(row, 2 * half - 1) >= half for half in LEVELS]
    row16 = row.astype(jnp.int16)
    his16 = [jnp.bitwise_and(row16, 2 * half - 1) >= half for half in LEVELS]
    tr = lax.broadcasted_iota(jnp.int32, (rows, rows), 0)
    tc = lax.broadcasted_iota(jnp.int32, (rows, rows), 1)
    tri = jnp.logical_and(tr >= tc, tr // CHUNK == tc // CHUNK).astype(BF16)
    nt_dims = (((1,), (1,)), ((), ()))
    heads = [slice(h * HEAD, (h + 1) * HEAD) for h in range(hpb)]
    chunks = [slice(u * CHUNK, (u + 1) * CHUNK) for u in range(per_trip)]

    def body(c, carry):
        r0 = pl.multiple_of(c * rows, rows)
        qp = q_ref[pl.ds(r0, rows), :].astype(F32)
        fp = f_ref[pl.ds(r0, rows), :].astype(F32)
        v = v_ref[pl.ds(r0, rows), :]
        gp = g_ref[pl.ds(r0, rows), :].astype(F32)
        sts = [st_ref[h] for h in range(hpb)]
        f = f_mid + f_amp * jnp.tanh(0.5 * fp)
        k = 1.0 - f
        q = _silu(qp)
        lf = jnp.log(f)
        lf_hi = lf.astype(BF16)
        lf_lo = (lf - lf_hi.astype(F32)).astype(BF16)
        b2 = jnp.dot(tri, jnp.concatenate([lf_hi, lf_lo], axis=1), preferred_element_type=F32)
        b = b2[:, :width] + b2[:, width:]
        qe = (q * jnp.exp(b)).astype(BF16)
        q16 = q.astype(BF16)
        k16 = k.astype(BF16)
        eye = lvl_ref[len(LEVELS)]
        scores = [[eye * lax.dot_general(q16[cs, sl], k16[cs, sl], nt_dims, preferred_element_type=F32)
                   for sl in heads] for cs in chunks]
        for i, half in enumerate(LEVELS):
            e = jnp.exp2((b - _level_ref_rows(b, half, row)) * jnp.where(his[i], LOG2E, -LOG2E))
            z = jnp.where(his16[i], q16, k16) * e.astype(BF16)
            mask = lvl_ref[i]
            for u, cs in enumerate(chunks):
                for h, sl in enumerate(heads):
                    gram = lax.dot_general(z[cs, sl], z[cs, sl], nt_dims, preferred_element_type=F32)
                    scores[u][h] = scores[u][h] + gram * mask
        blast = _row_bcast(b, CHUNK - 1, CHUNK)
        kd = (k * jnp.exp(blast - b)).astype(BF16)
        eblast = jnp.exp(blast)
        vt = [[v[cs, sl].astype(F32).T.astype(BF16) for sl in heads] for cs in chunks]
        out_rows = []
        for u, cs in enumerate(chunks):
            outs = []
            for h, sl in enumerate(heads):
                o = lax.dot_general(qe[cs, sl], sts[h].astype(BF16), nt_dims, preferred_element_type=F32)
                o = o + jnp.dot(scores[u][h].astype(BF16), v[cs, sl], preferred_element_type=F32)
                outs.append(_rms(o, gn[:, sl]))
                sts[h] = (sts[h] * eblast[u * CHUNK:u * CHUNK + 1, sl]
                          + jnp.dot(vt[u][h], kd[cs, sl], preferred_element_type=F32))
            out_rows.append(jnp.concatenate(outs, axis=1))
        o_all = jnp.concatenate(out_rows, axis=0) if per_trip > 1 else out_rows[0]
        o_ref[pl.ds(r0, rows), :] = (o_all * _silu(gp)).astype(o_ref.dtype)
        for h in range(hpb):
            st_ref[h] = sts[h]
        return carry

    lax.fori_loop(0, tb // rows, body, 0)
    for h in range(hpb):
        snew_ref[0, h] = st_ref[h].T


def _hgrn2(proj, nb, t, s0, lb_param, gn):
    nh = s0.shape[1]
    d = nh * HEAD
    hpb = 8
    cols = hpb * HEAD
    ng = nh // hpb
    tb = _pick(t, (1024, 512, 256, 128, 64))
    nt = t // tb
    per_trip = _pick(tb // CHUNK, (4, 2, 1))
    kern = functools.partial(_hgrn2_kernel, tb=tb, hpb=hpb, per_trip=per_trip)
    lvl = _hgrn2_masks()

    def col_spec(group):
        return pl.BlockSpec((tb, cols), lambda b, h, s: (b * nt + s, group * ng + h))

    return pl.pallas_call(
        kern,
        grid=(nb, ng, nt),
        in_specs=[
            col_spec(2), col_spec(3), col_spec(4), col_spec(5),
            pl.BlockSpec((1, hpb, HEAD, HEAD), lambda b, h, s: (b, h, 0, 0)),
            pl.BlockSpec((lb_param.shape[0], cols), lambda b, h, s: (0, h)),
            pl.BlockSpec((1, cols), lambda b, h, s: (0, h)),
            pl.BlockSpec(lvl.shape, lambda b, h, s: (0, 0, 0)),
        ],
        out_specs=[
            pl.BlockSpec((tb, cols), lambda b, h, s: (b * nt + s, h)),
            pl.BlockSpec((1, hpb, HEAD, HEAD), lambda b, h, s: (b, h, 0, 0)),
        ],
        out_shape=[
            jax.ShapeDtypeStruct((nb * t, d), BF16),
            jax.ShapeDtypeStruct(s0.shape, F32),
        ],
        scratch_shapes=[pltpu.VMEM((hpb, HEAD, HEAD), F32)],
        compiler_params=_cparams(("parallel", "parallel", "arbitrary")),
        name="hgrn2",
    )(proj, proj, proj, proj, s0, lb_param, gn, lvl)


def _merge_kernel(a_ref, b_ref, ga_ref, gb_ref, h_ref, g_ref, wa_ref, wb_ref, wo_ref, o_ref):
    br_a = jnp.dot(a_ref[...], wa_ref[...], preferred_element_type=F32)
    br_b = jnp.dot(b_ref[...], wb_ref[...], preferred_element_type=F32)
    m = _sigmoid(ga_ref[...].astype(F32)) * br_a + _sigmoid(gb_ref[...].astype(F32)) * br_b
    y = jnp.dot(m.astype(BF16), wo_ref[...], preferred_element_type=F32)
    o_ref[...] = h_ref[...] + _rms(y, g_ref[...])


def _merge(a_in, b_in, proj, h, g_post, w_a, w_b, w_o):
    m, d = h.shape
    tm = _pick(m, (256, 128))
    resident = dict(pipeline_mode=pl.Buffered(1))
    return pl.pallas_call(
        _merge_kernel,
        grid=(m // tm,),
        in_specs=[
            pl.BlockSpec((tm, d), lambda i: (i, 0)),
            pl.BlockSpec((tm, d), lambda i: (i, 0)),
            pl.BlockSpec((tm, d), lambda i: (i, 6)),
            pl.BlockSpec((tm, d), lambda i: (i, 7)),
            pl.BlockSpec((tm, d), lambda i: (i, 0)),
            pl.BlockSpec((1, d), lambda i: (0, 0)),
            pl.BlockSpec((d, d), lambda i: (0, 0), **resident),
            pl.BlockSpec((d, d), lambda i: (0, 0), **resident),
            pl.BlockSpec((d, d), lambda i: (0, 0), **resident),
        ],
        out_specs=pl.BlockSpec((tm, d), lambda i: (i, 0)),
        out_shape=jax.ShapeDtypeStruct((m, d), F32),
        compiler_params=_cparams(("parallel",)),
        name="merge",
    )(a_in, b_in, proj, proj, h, g_post, w_a, w_b, w_o)


def _ple_kernel(h_ref, p_ref, gpre_ref, gpost_ref, wg_ref, wp_ref, o_ref):
    h = h_ref[...]
    gate = jnp.dot(_rms(h, gpre_ref[...]).astype(BF16), wg_ref[...], preferred_element_type=F32)
    emb = jnp.dot(p_ref[...].astype(BF16), wp_ref[...], preferred_element_type=F32)
    o_ref[...] = h + _rms(emb * _sigmoid(gate), gpost_ref[...])


def _ple(h, p, g_pre, g_post, w_gate, w_proj):
    m, d = h.shape
    pd = p.shape[1]
    tm = _pick(m, (512, 256, 128))
    return pl.pallas_call(
        _ple_kernel,
        grid=(m // tm,),
        in_specs=[
            pl.BlockSpec((tm, d), lambda i: (i, 0)),
            pl.BlockSpec((tm, pd), lambda i: (i, 0)),
            pl.BlockSpec((1, d), lambda i: (0, 0)),
            pl.BlockSpec((1, d), lambda i: (0, 0)),
            pl.BlockSpec((d, d), lambda i: (0, 0)),
            pl.BlockSpec((pd, d), lambda i: (0, 0)),
        ],
        out_specs=pl.BlockSpec((tm, d), lambda i: (i, 0)),
        out_shape=jax.ShapeDtypeStruct((m, d), F32),
        compiler_params=_cparams(("parallel",)),
        name="ple",
    )(h, p, g_pre, g_post, w_gate, w_proj)


def _layer(x, p, conv_buf, h0, s0, reset_first, w):
    nb, t, d = x.shape
    m = nb * t
    xf = x.reshape(m, d)
    h1 = _ffn(xf, w["g_ffn1_pre"], w["g_ffn1_post"], w["w_ffn1_in"], w["w_ffn1_out"])
    proj = _proj(h1, w["g_mix_pre"], w["w_in"])
    cbuf = jnp.concatenate(
        [jnp.zeros((nb, SUBLANES - (CONV_W - 1), d), F32), conv_buf.astype(F32)], axis=1)
    a_in, h_last = _rglru(proj, nb, t, cbuf, h0, w["rg_conv_w"], w["rg_conv_b"],
                          w["rg_wax"], w["rg_bax"], w["rg_a_param"], reset_first)
    b_in, s_new = _hgrn2(proj, nb, t, s0, w["hg_lower_bound"], w["hg_norm_g"])
    h2 = _merge(a_in, b_in, proj, h1, w["g_mix_post"], w["w_branch_a"], w["w_branch_b"], w["w_out"])
    h3 = _ffn(h2, w["g_ffn2_pre"], w["g_ffn2_post"], w["w_ffn2_in"], w["w_ffn2_out"])
    y = _ple(h3, p.reshape(m, p.shape[-1]), w["g_ple_pre"], w["g_ple_post"], w["w_ple_gate"], w["w_ple_proj"])
    conv_new = proj.reshape(nb, t, -1)[:, t - (CONV_W - 1):, :d].astype(F32)
    return y.reshape(nb, t, d), conv_new, h_last, s_new


def kernel(x_prompt, x_sample, p_prompt, p_sample, state_rglru_conv, state_rglru_h, state_hgrn2, g_ffn1_pre, g_ffn1_post, w_ffn1_in, w_ffn1_out, g_mix_pre, g_mix_post, w_in, rg_conv_w, rg_conv_b, rg_w_a, rg_b_a, rg_w_x, rg_b_x, rg_a_param, hg_lower_bound, hg_norm_g, w_branch_a, w_branch_b, w_out, g_ffn2_pre, g_ffn2_post, w_ffn2_in, w_ffn2_out, g_ple_pre, g_ple_post, w_ple_gate, w_ple_proj):
    assert g_ffn1_pre.shape[0] == 1 and hg_lower_bound.shape[0] == 2, "written for a single trunk layer"
    assert x_prompt.shape[1] >= CONV_W - 1 and x_sample.shape[1] >= CONV_W - 1
    l = 0
    bf = lambda a: a.astype(BF16)
    w = {
        "g_ffn1_pre": g_ffn1_pre[l][None], "g_ffn1_post": g_ffn1_post[l][None],
        "w_ffn1_in": bf(w_ffn1_in[l]), "w_ffn1_out": bf(w_ffn1_out[l]),
        "g_mix_pre": g_mix_pre[l][None], "g_mix_post": g_mix_post[l][None],
        "w_in": bf(w_in[l]),
        "rg_conv_w": rg_conv_w[l], "rg_conv_b": rg_conv_b[l][None],
        "rg_wax": bf(jnp.concatenate([rg_w_a[l], rg_w_x[l]], axis=-1)),
        "rg_bax": jnp.concatenate([rg_b_a[l], rg_b_x[l]], axis=-1)[:, None, :],
        "rg_a_param": rg_a_param[l][None],
        "hg_lower_bound": hg_lower_bound, "hg_norm_g": hg_norm_g[l][None],
        "w_branch_a": bf(w_branch_a[l]), "w_branch_b": bf(w_branch_b[l]), "w_out": bf(w_out[l]),
        "g_ffn2_pre": g_ffn2_pre[l][None], "g_ffn2_post": g_ffn2_post[l][None],
        "w_ffn2_in": bf(w_ffn2_in[l]), "w_ffn2_out": bf(w_ffn2_out[l]),
        "g_ple_pre": g_ple_pre[l][None], "g_ple_post": g_ple_post[l][None],
        "w_ple_gate": bf(w_ple_gate[l]), "w_ple_proj": bf(w_ple_proj[l]),
    }
    bp = x_prompt.shape[0]
    d = x_prompt.shape[-1]
    nh = state_hgrn2.shape[2]
    yp, conv_p, h_p, s_p = _layer(
        x_prompt, p_prompt[l],
        jnp.zeros((bp, CONV_W - 1, d), F32), jnp.zeros((bp, d), F32),
        jnp.zeros((bp, nh, HEAD, HEAD), F32), True, w)
    ys, conv_s, h_s, s_s = _layer(
        x_sample, p_sample[l], state_rglru_conv[l], state_rglru_h[l], state_hgrn2[l], False, w)
    return (yp, ys, conv_p[None], h_p[None], s_p[None], conv_s[None], h_s[None], s_s[None])
```

```python
import functools

import jax
import jax.numpy as jnp
import numpy as np
from jax import lax
from jax.experimental import pallas as pl
from jax.experimental.pallas import tpu as pltpu

F32 = jnp.float32
BF16 = jnp.bfloat16

EPS = 1e-6
RG_C = 8.0
CONV_W = 4
HEAD = 128
CHUNK = 64
SUBLANES = 8
VMEM_LIMIT = 56 * 1024 * 1024
LOG2E = 1.4426950408889634


def _cparams(sem, flags=None):
    return pltpu.CompilerParams(dimension_semantics=sem, vmem_limit_bytes=VMEM_LIMIT, flags=flags)


def _rms(x, g):
    return x * lax.rsqrt(jnp.mean(x * x, axis=-1, keepdims=True) + EPS) * g


def _sigmoid(x):
    return 0.5 + 0.5 * jnp.tanh(0.5 * x)


def _silu(x):
    h = 0.5 * x
    return h + h * jnp.tanh(h)


def _gelu_tanh(x):
    h = 0.5 * x
    return h + h * jnp.tanh(x * (0.7978845608028654 + (0.7978845608028654 * 0.044715) * (x * x)))


def _log1p(z):
    w = 1.0 + z
    return jnp.where(w == 1.0, z, jnp.log(w) * (z / jnp.where(w == 1.0, 1.0, w - 1.0)))


def _pick(n, prefs):
    for p in prefs:
        if n % p == 0:
            return p
    raise ValueError(f"no tile for {n} in {prefs}")


def _ffn_up_kernel(x_ref, gpre_ref, wg_ref, wu_ref, o_ref, xn_ref):
    @pl.when(pl.program_id(1) == 0)
    def _():
        xn_ref[...] = _rms(x_ref[...], gpre_ref[...]).astype(BF16)

    xn = xn_ref[...]
    gate = jnp.dot(xn, wg_ref[...], preferred_element_type=F32)
    up = jnp.dot(xn, wu_ref[...], preferred_element_type=F32)
    o_ref[...] = (_silu(gate) * up).astype(o_ref.dtype)


def _ffn_down_kernel(a_ref, x_ref, gpost_ref, wo_ref, o_ref):
    y = jnp.dot(a_ref[...], wo_ref[...], preferred_element_type=F32)
    o_ref[...] = x_ref[...] + 0.5 * _rms(y, gpost_ref[...])


def _ffn(x, g_pre, g_post, w_in, w_out):
    m, d = x.shape
    f = w_out.shape[0]
    tm = _pick(m, (1024, 512, 256, 128))
    tf = _pick(f, (512, 256, 128))
    nf = f // tf
    act = pl.pallas_call(
        _ffn_up_kernel,
        grid=(m // tm, nf),
        in_specs=[
            pl.BlockSpec((tm, d), lambda i, j: (i, 0)),
            pl.BlockSpec((1, d), lambda i, j: (0, 0)),
            pl.BlockSpec((d, tf), lambda i, j: (0, j)),
            pl.BlockSpec((d, tf), lambda i, j: (0, j + nf)),
        ],
        out_specs=pl.BlockSpec((tm, tf), lambda i, j: (i, j)),
        out_shape=jax.ShapeDtypeStruct((m, f), BF16),
        scratch_shapes=[pltpu.VMEM((tm, d), BF16)],
        compiler_params=_cparams(("parallel", "arbitrary")),
        name="ffn_up",
    )(x, g_pre, w_in, w_in)
    tm2 = _pick(m, (512, 256, 128))
    return pl.pallas_call(
        _ffn_down_kernel,
        grid=(m // tm2,),
        in_specs=[
            pl.BlockSpec((tm2, f), lambda i: (i, 0)),
            pl.BlockSpec((tm2, d), lambda i: (i, 0)),
            pl.BlockSpec((1, d), lambda i: (0, 0)),
            pl.BlockSpec((f, d), lambda i: (0, 0), pipeline_mode=pl.Buffered(1)),
        ],
        out_specs=pl.BlockSpec((tm2, d), lambda i: (i, 0)),
        out_shape=jax.ShapeDtypeStruct((m, d), F32),
        compiler_params=_cparams(("parallel",)),
        name="ffn_down",
    )(act, x, g_post, w_out)


def _proj_kernel(x_ref, g_ref, w_ref, o_ref, xn_ref):
    @pl.when(pl.program_id(1) == 0)
    def _():
        xn_ref[...] = _rms(x_ref[...], g_ref[...]).astype(BF16)

    o_ref[...] = jnp.dot(xn_ref[...], w_ref[...], preferred_element_type=F32).astype(o_ref.dtype)


def _proj(x, g, w):
    m, d = x.shape
    n = w.shape[1]
    tm = _pick(m, (1024, 512, 256, 128))
    tn = _pick(n, (2048, 1024, 512, 256, 128))
    return pl.pallas_call(
        _proj_kernel,
        grid=(m // tm, n // tn),
        in_specs=[
            pl.BlockSpec((tm, d), lambda i, j: (i, 0)),
            pl.BlockSpec((1, d), lambda i, j: (0, 0)),
            pl.BlockSpec((d, tn), lambda i, j: (0, j)),
        ],
        out_specs=pl.BlockSpec((tm, tn), lambda i, j: (i, j)),
        out_shape=jax.ShapeDtypeStruct((m, n), BF16),
        scratch_shapes=[pltpu.VMEM((tm, d), BF16)],
        compiler_params=_cparams(("parallel", "arbitrary")),
        name="proj",
    )(x, g, w)


RG_GROUP = 16
RG_PAD = SUBLANES


def _rglru_kernel(x_ref, y_ref, cbuf_ref, h0_ref, cw_ref, cb_ref, wax_ref, bax_ref, ap_ref,
                  o_ref, hlast_ref, xp_ref, tail_ref, a_ref, u_ref, ix0_ref, carry_ref,
                  *, tb, cols, reset_first):
    t = pl.program_id(2)
    pad = SUBLANES
    nslab = cols // HEAD
    pitch = tb + RG_PAD
    slabs = [slice(h * HEAD, (h + 1) * HEAD) for h in range(nslab)]

    @pl.when(t == 0)
    def _():
        tail_ref[...] = cbuf_ref[...]
        carry_ref[...] = h0_ref[...]

    cw = cw_ref[...]
    cb = cb_ref[...]
    slope = (-RG_C) * (jnp.maximum(ap_ref[...], 0.0) + _log1p(jnp.exp(-jnp.abs(ap_ref[...]))))

    def gates_of(seq, carry):
        xp_ref[0:pad, :] = tail_ref[seq]
        xp_ref[pad:pad + tb, :] = x_ref[seq].astype(F32)
        xc = cb + cw[0:1, :] * xp_ref[pad:pad + tb, :]
        for j in range(1, CONV_W):
            xc = xc + cw[j:j + 1, :] * xp_ref[pad - j:pad - j + tb, :]
        tail_ref[seq] = xp_ref[tb:tb + pad, :]
        r0 = pl.multiple_of(seq * pitch, SUBLANES)
        for h, sl in enumerate(slabs):
            xh = xc[:, sl]
            gates = jnp.dot(xh.astype(BF16), wax_ref[h], preferred_element_type=F32) + bax_ref[h]
            r = _sigmoid(gates[:, :HEAD])
            ix = _sigmoid(gates[:, HEAD:]) * xh
            av = jnp.exp(r * slope[:, sl])
            gap = 1.0 - av * av
            mult = jnp.where(gap > 0.0, gap * lax.rsqrt(gap), 0.0)
            a_ref[h, pl.ds(r0, tb), :] = av
            u_ref[h, pl.ds(r0, tb), :] = mult * ix
            if reset_first:
                ix0_ref[h, pl.ds(seq, 1), :] = ix[0:1, :]
        return carry

    lax.fori_loop(0, RG_GROUP, gates_of, 0)

    if reset_first:
        @pl.when(t == 0)
        def _():
            for h in range(nslab):
                a_ref[h, pl.ds(0, RG_GROUP, stride=pitch), :] = jnp.zeros((RG_GROUP, HEAD), F32)
                u_ref[h, pl.ds(0, RG_GROUP, stride=pitch), :] = ix0_ref[h]

    def step(tt, hs):
        out = []
        for h in range(nslab):
            rows = pl.ds(tt, RG_GROUP, stride=pitch)
            hcur = a_ref[h, rows, :] * hs[h] + u_ref[h, rows, :]
            u_ref[h, rows, :] = hcur
            out.append(hcur)
        return tuple(out)

    hs = lax.fori_loop(0, tb, step, tuple(carry_ref[:, sl] for sl in slabs), unroll=8)
    last = jnp.concatenate(hs, axis=1)
    carry_ref[...] = last
    hlast_ref[...] = last

    def emit(seq, carry):
        r0 = pl.multiple_of(seq * pitch, SUBLANES)
        hseq = jnp.concatenate([u_ref[h, pl.ds(r0, tb), :] for h in range(nslab)], axis=1)
        o_ref[seq] = (_gelu_tanh(y_ref[seq].astype(F32)) * hseq).astype(o_ref.dtype)
        return carry

    lax.fori_loop(0, RG_GROUP, emit, 0)


def _rglru(proj, nb, t, conv_buf, h0, cw, cb, wax, bax, a_param, reset_first):
    d = cw.shape[1]
    cols = 512
    nslab = cols // HEAD
    tb = _pick(t, (256, 128, 64))
    nt = t // tb
    ncol = d // cols
    assert nb % RG_GROUP == 0
    kern = functools.partial(_rglru_kernel, tb=tb, cols=cols, reset_first=reset_first)
    proj3 = proj.reshape(nb, t, proj.shape[1])
    out, hlast = pl.pallas_call(
        kern,
        grid=(nb // RG_GROUP, ncol, nt),
        in_specs=[
            pl.BlockSpec((RG_GROUP, tb, cols), lambda g, c, s: (g, s, c)),
            pl.BlockSpec((RG_GROUP, tb, cols), lambda g, c, s: (g, s, ncol + c)),
            pl.BlockSpec((RG_GROUP, SUBLANES, cols), lambda g, c, s: (g, 0, c)),
            pl.BlockSpec((RG_GROUP, cols), lambda g, c, s: (g, c)),
            pl.BlockSpec((CONV_W, cols), lambda g, c, s: (0, c)),
            pl.BlockSpec((1, cols), lambda g, c, s: (0, c)),
            pl.BlockSpec((nslab, HEAD, 2 * HEAD), lambda g, c, s: (c, 0, 0)),
            pl.BlockSpec((nslab, 1, 2 * HEAD), lambda g, c, s: (c, 0, 0)),
            pl.BlockSpec((1, cols), lambda g, c, s: (0, c)),
        ],
        out_specs=[
            pl.BlockSpec((RG_GROUP, tb, cols), lambda g, c, s: (g, s, c)),
            pl.BlockSpec((RG_GROUP, cols), lambda g, c, s: (g, c)),
        ],
        out_shape=[
            jax.ShapeDtypeStruct((nb, t, d), BF16),
            jax.ShapeDtypeStruct((nb, d), F32),
        ],
        scratch_shapes=[
            pltpu.VMEM((tb + SUBLANES, cols), F32),
            pltpu.VMEM((RG_GROUP, SUBLANES, cols), F32),
            pltpu.VMEM((nslab, RG_GROUP * (tb + RG_PAD), HEAD), F32),
            pltpu.VMEM((nslab, RG_GROUP * (tb + RG_PAD), HEAD), F32),
            pltpu.VMEM((nslab, RG_GROUP, HEAD), F32),
            pltpu.VMEM((RG_GROUP, cols), F32),
        ],
        compiler_params=_cparams(("parallel", "parallel", "arbitrary")),
        name="rglru",
    )(proj3, proj3, conv_buf, h0, cw, cb, wax, bax, a_param)
    return out.reshape(nb * t, d), hlast


LEVELS = (32, 16, 8, 4, 2, 1)


def _hgrn2_masks():
    t = np.arange(CHUNK)[:, None]
    s = np.arange(CHUNK)[None, :]
    out = []
    for half in LEVELS:
        same = (t // (2 * half)) == (s // (2 * half))
        out.append(same & (t % (2 * half) >= half) & (s % (2 * half) < half))
    out.append(t == s)
    return jnp.asarray(np.stack(out), F32)


def _row_bcast(b, first, step):
    span = max(step, SUBLANES)
    return jnp.concatenate(
        [jnp.broadcast_to(b[r:r + 1, :], (span, b.shape[1])) for r in range(first, b.shape[0], span)], axis=0)


def _level_ref_rows(b, half, row):
    if 2 * half >= SUBLANES:
        return _row_bcast(b, half - 1, 2 * half)
    if half == 2:
        return jnp.where(jnp.bitwise_and(row, 4) == 0, _row_bcast(b, 1, SUBLANES), _row_bcast(b, 5, SUBLANES))
    assert half == 1
    return jnp.where(jnp.bitwise_and(row, 1) == 0, b, pltpu.roll(b, 1, 0))


def _hgrn2_kernel(q_ref, f_ref, v_ref, g_ref, s0_ref, lbp_ref, gn_ref, lvl_ref,
                  o_ref, snew_ref, st_ref, *, tb, hpb, per_trip):
    t = pl.program_id(2)
    width = hpb * HEAD
    rows = per_trip * CHUNK

    @pl.when(t == 0)
    def _():
        for h in range(hpb):
            st_ref[h] = s0_ref[0, h].T

    lbp = lbp_ref[...]
    le = jnp.exp(lbp - jnp.max(lbp, axis=0, keepdims=True))
    lb = le[0:1, :] / jnp.sum(le, axis=0, keepdims=True)
    f_amp = 0.5 * (1.0 - lb)
    gn = gn_ref[...]
    row = lax.broadcasted_iota(jnp.int32, (rows, width), 0)
    his = [jnp.bitwise_and(row, 2 * half - 1) >= half for half in LEVELS]
    row16 = row.astype(jnp.int16)
    his16 = [jnp.bitwise_and(row16, 2 * half - 1) >= half for half in LEVELS]
    tr = lax.broadcasted_iota(jnp.int32, (rows, rows), 0)
    tc = lax.broadcasted_iota(jnp.int32, (rows, rows), 1)
    tri = jnp.logical_and(tr >= tc, tr // CHUNK == tc // CHUNK).astype(BF16)
    nt_dims = (((1,), (1,)), ((), ()))
    heads = [slice(h * HEAD, (h + 1) * HEAD) for h in range(hpb)]
    chunks = [slice(u * CHUNK, (u + 1) * CHUNK) for u in range(per_trip)]

    def body(c, carry):
        r0 = pl.multiple_of(c * rows, rows)
        qp = q_ref[pl.ds(r0, rows), :].astype(F32)
        fp = f_ref[pl.ds(r0, rows), :].astype(F32)
        v = v_ref[pl.ds(r0, rows), :]
        gp = g_ref[pl.ds(r0, rows), :].astype(F32)
        sts = [st_ref[h] for h in range(hpb)]
        f = lb + f_amp * (1.0 + jnp.tanh(0.5 * fp))
        k = 1.0 - f
        q = _silu(qp)
        lf = jnp.log(f)
        lf_hi = lf.astype(BF16)
        lf_lo = (lf - lf_hi.astype(F32)).astype(BF16)
        b2 = jnp.dot(tri, jnp.concatenate([lf_hi, lf_lo], axis=1), preferred_element_type=F32)
        b = b2[:, :width] + b2[:, width:]
        qe = (q * jnp.exp(b)).astype(BF16)
        q16 = q.astype(BF16)
        k16 = k.astype(BF16)
        eye = lvl_ref[len(LEVELS)].astype(BF16)
        scores = [[eye * lax.dot_general(q16[cs, sl], k16[cs, sl], nt_dims,
                                         preferred_element_type=F32).astype(BF16)
                   for sl in heads] for cs in chunks]
        for i, half in enumerate(LEVELS):
            e = jnp.exp2((b - _level_ref_rows(b, half, row)) * jnp.where(his[i], LOG2E, -LOG2E))
            z = jnp.where(his16[i], q16, k16) * e.astype(BF16)
            mask = lvl_ref[i].astype(BF16)
            for u, cs in enumerate(chunks):
                for h, sl in enumerate(heads):
                    gram = lax.dot_general(z[cs, sl], z[cs, sl], nt_dims, preferred_element_type=F32)
                    scores[u][h] = scores[u][h] + gram.astype(BF16) * mask
        blast = _row_bcast(b, CHUNK - 1, CHUNK)
        kd = (k * jnp.exp(blast - b)).astype(BF16)
        eblast = jnp.exp(blast)
        vt = [[v[cs, sl].astype(F32).T.astype(BF16) for sl in heads] for cs in chunks]
        out_rows = []
        for u, cs in enumerate(chunks):
            outs = []
            for h, sl in enumerate(heads):
                o = lax.dot_general(qe[cs, sl], sts[h].astype(BF16), nt_dims, preferred_element_type=F32)
                o = o + jnp.dot(scores[u][h], v[cs, sl], preferred_element_type=F32)
                outs.append(_rms(o, gn[:, sl]))
                sts[h] = (sts[h] * eblast[u * CHUNK:u * CHUNK + 1, sl]
                          + jnp.dot(vt[u][h], kd[cs, sl], preferred_element_type=F32))
            out_rows.append(jnp.concatenate(outs, axis=1))
        o_all = jnp.concatenate(out_rows, axis=0) if per_trip > 1 else out_rows[0]
        o_ref[pl.ds(r0, rows), :] = (o_all * _silu(gp)).astype(o_ref.dtype)
        for h in range(hpb):
            st_ref[h] = sts[h]
        return carry

    lax.fori_loop(0, tb // rows, body, 0)

    @pl.when(t == pl.num_programs(2) - 1)
    def _():
        for h in range(hpb):
            snew_ref[0, h] = st_ref[h].T


def _hgrn2(proj, nb, t, s0, lb_param, gn):
    nh = s0.shape[1]
    d = nh * HEAD
    hpb = 8
    cols = hpb * HEAD
    ng = nh // hpb
    tb = _pick(t, (1024, 512, 256, 128, 64))
    nt = t // tb
    per_trip = _pick(tb // CHUNK, (4, 2, 1))
    kern = functools.partial(_hgrn2_kernel, tb=tb, hpb=hpb, per_trip=per_trip)
    lvl = _hgrn2_masks()

    def col_spec(group):
        return pl.BlockSpec((tb, cols), lambda b, h, s: (b * nt + s, group * ng + h))

    return pl.pallas_call(
        kern,
        grid=(nb, ng, nt),
        in_specs=[
            col_spec(2), col_spec(3), col_spec(4), col_spec(5),
            pl.BlockSpec((1, hpb, HEAD, HEAD), lambda b, h, s: (b, h, 0, 0)),
            pl.BlockSpec((lb_param.shape[0], cols), lambda b, h, s: (0, h)),
            pl.BlockSpec((1, cols), lambda b, h, s: (0, h)),
            pl.BlockSpec(lvl.shape, lambda b, h, s: (0, 0, 0)),
        ],
        out_specs=[
            pl.BlockSpec((tb, cols), lambda b, h, s: (b * nt + s, h)),
            pl.BlockSpec((1, hpb, HEAD, HEAD), lambda b, h, s: (b, h, 0, 0)),
        ],
        out_shape=[
            jax.ShapeDtypeStruct((nb * t, d), BF16),
            jax.ShapeDtypeStruct(s0.shape, F32),
        ],
        scratch_shapes=[pltpu.VMEM((hpb, HEAD, HEAD), F32)],
        compiler_params=_cparams(("parallel", "parallel", "arbitrary")),
        name="hgrn2",
    )(proj, proj, proj, proj, s0, lb_param, gn, lvl)


def _merge_kernel(a_ref, b_ref, ga_ref, gb_ref, h_ref, g_ref, wa_ref, wb_ref, wo_ref, o_ref):
    br_a = jnp.dot(a_ref[...], wa_ref[...], preferred_element_type=F32)
    br_b = jnp.dot(b_ref[...], wb_ref[...], preferred_element_type=F32)
    m = _sigmoid(ga_ref[...].astype(F32)) * br_a + _sigmoid(gb_ref[...].astype(F32)) * br_b
    y = jnp.dot(m.astype(BF16), wo_ref[...], preferred_element_type=F32)
    o_ref[...] = h_ref[...] + _rms(y, g_ref[...])


def _merge(a_in, b_in, proj, h, g_post, w_a, w_b, w_o):
    m, d = h.shape
    tm = _pick(m, (256, 128))
    resident = dict(pipeline_mode=pl.Buffered(1))
    return pl.pallas_call(
        _merge_kernel,
        grid=(m // tm,),
        in_specs=[
            pl.BlockSpec((tm, d), lambda i: (i, 0)),
            pl.BlockSpec((tm, d), lambda i: (i, 0)),
            pl.BlockSpec((tm, d), lambda i: (i, 6)),
            pl.BlockSpec((tm, d), lambda i: (i, 7)),
            pl.BlockSpec((tm, d), lambda i: (i, 0)),
            pl.BlockSpec((1, d), lambda i: (0, 0)),
            pl.BlockSpec((d, d), lambda i: (0, 0), **resident),
            pl.BlockSpec((d, d), lambda i: (0, 0), **resident),
            pl.BlockSpec((d, d), lambda i: (0, 0), **resident),
        ],
        out_specs=pl.BlockSpec((tm, d), lambda i: (i, 0)),
        out_shape=jax.ShapeDtypeStruct((m, d), F32),
        compiler_params=_cparams(("parallel",)),
        name="merge",
    )(a_in, b_in, proj, proj, h, g_post, w_a, w_b, w_o)


def _ple_kernel(h_ref, p_ref, gpre_ref, gpost_ref, wg_ref, wp_ref, o_ref):
    h = h_ref[...]
    gate = jnp.dot(_rms(h, gpre_ref[...]).astype(BF16), wg_ref[...], preferred_element_type=F32)
    emb = jnp.dot(p_ref[...].astype(BF16), wp_ref[...], preferred_element_type=F32)
    o_ref[...] = h + _rms(emb * _sigmoid(gate), gpost_ref[...])


def _ple(h, p, g_pre, g_post, w_gate, w_proj):
    m, d = h.shape
    pd = p.shape[1]
    tm = _pick(m, (512, 256, 128))
    return pl.pallas_call(
        _ple_kernel,
        grid=(m // tm,),
        in_specs=[
            pl.BlockSpec((tm, d), lambda i: (i, 0)),
            pl.BlockSpec((tm, pd), lambda i: (i, 0)),
            pl.BlockSpec((1, d), lambda i: (0, 0)),
            pl.BlockSpec((1, d), lambda i: (0, 0)),
            pl.BlockSpec((d, d), lambda i: (0, 0)),
            pl.BlockSpec((pd, d), lambda i: (0, 0)),
        ],
        out_specs=pl.BlockSpec((tm, d), lambda i: (i, 0)),
        out_shape=jax.ShapeDtypeStruct((m, d), F32),
        compiler_params=_cparams(("parallel",)),
        name="ple",
    )(h, p, g_pre, g_post, w_gate, w_proj)


def _layer(x, p, conv_buf, h0, s0, reset_first, w):
    nb, t, d = x.shape
    m = nb * t
    xf = x.reshape(m, d)
    h1 = _ffn(xf, w["g_ffn1_pre"], w["g_ffn1_post"], w["w_ffn1_in"], w["w_ffn1_out"])
    proj = _proj(h1, w["g_mix_pre"], w["w_in"])
    cbuf = jnp.concatenate(
        [jnp.zeros((nb, SUBLANES - (CONV_W - 1), d), F32), conv_buf.astype(F32)], axis=1)
    a_in, h_last = _rglru(proj, nb, t, cbuf, h0, w["rg_conv_w"], w["rg_conv_b"],
                          w["rg_wax"], w["rg_bax"], w["rg_a_param"], reset_first)
    b_in, s_new = _hgrn2(proj, nb, t, s0, w["hg_lower_bound"], w["hg_norm_g"])
    h2 = _merge(a_in, b_in, proj, h1, w["g_mix_post"], w["w_branch_a"], w["w_branch_b"], w["w_out"])
    h3 = _ffn(h2, w["g_ffn2_pre"], w["g_ffn2_post"], w["w_ffn2_in"], w["w_ffn2_out"])
    y = _ple(h3, p.reshape(m, p.shape[-1]), w["g_ple_pre"], w["g_ple_post"], w["w_ple_gate"], w["w_ple_proj"])
    conv_new = proj.reshape(nb, t, -1)[:, t - (CONV_W - 1):, :d].astype(F32)
    return y.reshape(nb, t, d), conv_new, h_last, s_new


def kernel(x_prompt, x_sample, p_prompt, p_sample, state_rglru_conv, state_rglru_h, state_hgrn2, g_ffn1_pre, g_ffn1_post, w_ffn1_in, w_ffn1_out, g_mix_pre, g_mix_post, w_in, rg_conv_w, rg_conv_b, rg_w_a, rg_b_a, rg_w_x, rg_b_x, rg_a_param, hg_lower_bound, hg_norm_g, w_branch_a, w_branch_b, w_out, g_ffn2_pre, g_ffn2_post, w_ffn2_in, w_ffn2_out, g_ple_pre, g_ple_post, w_ple_gate, w_ple_proj):
    assert g_ffn1_pre.shape[0] == 1 and hg_lower_bound.shape[0] == 2, "written for a single trunk layer"
    assert x_prompt.shape[1] >= CONV_W - 1 and x_sample.shape[1] >= CONV_W - 1
    l = 0
    bf = lambda a: a.astype(BF16)
    w = {
        "g_ffn1_pre": g_ffn1_pre[l][None], "g_ffn1_post": g_ffn1_post[l][None],
        "w_ffn1_in": bf(w_ffn1_in[l]), "w_ffn1_out": bf(w_ffn1_out[l]),
        "g_mix_pre": g_mix_pre[l][None], "g_mix_post": g_mix_post[l][None],
        "w_in": bf(w_in[l]),
        "rg_conv_w": rg_conv_w[l], "rg_conv_b": rg_conv_b[l][None],
        "rg_wax": bf(jnp.concatenate([rg_w_a[l], rg_w_x[l]], axis=-1)),
        "rg_bax": jnp.concatenate([rg_b_a[l], rg_b_x[l]], axis=-1)[:, None, :],
        "rg_a_param": rg_a_param[l][None],
        "hg_lower_bound": hg_lower_bound, "hg_norm_g": hg_norm_g[l][None],
        "w_branch_a": bf(w_branch_a[l]), "w_branch_b": bf(w_branch_b[l]), "w_out": bf(w_out[l]),
        "g_ffn2_pre": g_ffn2_pre[l][None], "g_ffn2_post": g_ffn2_post[l][None],
        "w_ffn2_in": bf(w_ffn2_in[l]), "w_ffn2_out": bf(w_ffn2_out[l]),
        "g_ple_pre": g_ple_pre[l][None], "g_ple_post": g_ple_post[l][None],
        "w_ple_gate": bf(w_ple_gate[l]), "w_ple_proj": bf(w_ple_proj[l]),
    }
    bp = x_prompt.shape[0]
    d = x_prompt.shape[-1]
    nh = state_hgrn2.shape[2]
    yp, conv_p, h_p, s_p = _layer(
        x_prompt, p_prompt[l],
        jnp.zeros((bp, CONV_W - 1, d), F32), jnp.zeros((bp, d), F32),
        jnp.zeros((bp, nh, HEAD, HEAD), F32), True, w)
    ys, conv_s, h_s, s_s = _layer(
        x_sample, p_sample[l], state_rglru_conv[l], state_rglru_h[l], state_hgrn2[l], False, w)
    return (yp, ys, conv_p[None], h_p[None], s_p[None], conv_s[None], h_s[None], s_s[None])
```
